```python
import jax, jax.numpy as jnp
from jax import lax
import numpy as np

D_MODEL = 1024
BATCH = 8
SEQ = 4096
DEPTH = 1

CHUNK = 64
N_LEFT_CHUNKS = 8
BAND = (N_LEFT_CHUNKS + 1) * CHUNK
D_MIX = D_MODEL
D_ATTN = D_MIX // 2
HEAD_DIM = 64
N_HEADS = D_ATTN // HEAD_DIM
D_POOL = D_MIX - D_ATTN
POOL_WINDOWS = (2, 4, 8, 16)
N_POOL_GROUPS = len(POOL_WINDOWS)
POOL_GROUP_DIM = D_POOL // N_POOL_GROUPS
REL_CLIP = 128
N_REL = 2 * REL_CLIP + 1
D_FF = 2816
D_IN = 3 * D_ATTN + D_POOL
EPS = 1e-6
NEG_INF = -1e30

kernel_name = "hybrid_chunk_attn_pool_macaron"


def rmsnorm(x, g):
    xf = x.astype(jnp.float32)
    y = xf * lax.rsqrt(jnp.mean(xf * xf, axis=-1, keepdims=True) + EPS)
    return (y * g.astype(jnp.float32)).astype(x.dtype)


def swiglu_ffn(x, w_gate, w_up, w_down):
    return (jax.nn.silu(x @ w_gate) * (x @ w_up)) @ w_down


def multiscale_pool(u, pool_w, pool_scale):
    B, S, _ = u.shape
    uf = u.astype(jnp.float32)
    cs = jnp.cumsum(uf, axis=1)
    cs = jnp.concatenate([jnp.zeros_like(cs[:, :1]), cs], axis=1)
    t = jnp.arange(S)
    outs = []
    for g, w in enumerate(POOL_WINDOWS):
        lo, hi = g * POOL_GROUP_DIM, (g + 1) * POOL_GROUP_DIM
        c = cs[:, :, lo:hi]
        start = jnp.maximum(t + 1 - w, 0)
        win_sum = c[:, 1:] - c[:, start]
        count = (t + 1 - start).astype(jnp.float32)
        outs.append(win_sum / count[None, :, None] - uf[:, :, lo:hi])
    d = jnp.stack(outs, axis=2)
    y = jnp.einsum('bsgc,gcd->bsgd', d, pool_w.astype(jnp.float32)).reshape(B, S, D_POOL)
    return (y * pool_scale.astype(jnp.float32)).astype(u.dtype)


def chunked_attention(q, k, v, q_gain, k_gain, rel_bias):
    B, S, _ = q.shape
    nc = S // CHUNK
    q = rmsnorm(q.reshape(B, S, N_HEADS, HEAD_DIM), q_gain)
    k = rmsnorm(k.reshape(B, S, N_HEADS, HEAD_DIM), k_gain)
    v = v.reshape(B, S, N_HEADS, HEAD_DIM)
    pad = N_LEFT_CHUNKS * CHUNK
    kp = jnp.pad(k, ((0, 0), (pad, 0), (0, 0), (0, 0)))
    vp = jnp.pad(v, ((0, 0), (pad, 0), (0, 0), (0, 0)))
    qc = q.reshape(B, nc, CHUNK, N_HEADS, HEAD_DIM)
    qpos = jnp.arange(CHUNK)[:, None] + pad
    kpos = jnp.arange(BAND)[None, :]
    rel_idx = jnp.clip(qpos - kpos, -REL_CLIP, REL_CLIP) + REL_CLIP
    bias = rel_bias.astype(jnp.float32)[:, rel_idx]
    scale = HEAD_DIM ** -0.5

    def one_chunk(c):
        qb = lax.dynamic_index_in_dim(qc, c, axis=1, keepdims=False)
        kb = lax.dynamic_slice_in_dim(kp, c * CHUNK, BAND, axis=1)
        vb = lax.dynamic_slice_in_dim(vp, c * CHUNK, BAND, axis=1)
        s = jnp.einsum('bqhd,bkhd->bhqk', qb, kb).astype(jnp.float32) * scale + bias
        valid = (c * CHUNK + kpos - pad) >= 0
        s = jnp.where(valid[None, None], s, NEG_INF)
        p = jax.nn.softmax(s, axis=-1)
        return jnp.einsum('bhqk,bkhd->bqhd', p.astype(vb.dtype), vb)

    o = lax.map(one_chunk, jnp.arange(nc))
    return o.transpose(1, 0, 2, 3, 4).reshape(B, S, D_ATTN)


def hybrid_mixer(h, w_in, q_gain, k_gain, rel_bias, pool_w, pool_scale, w_out):
    proj = h @ w_in
    q = proj[..., :D_ATTN]
    k = proj[..., D_ATTN:2 * D_ATTN]
    v = proj[..., 2 * D_ATTN:3 * D_ATTN]
    u = proj[..., 3 * D_ATTN:]
    a = chunked_attention(q, k, v, q_gain, k_gain, rel_bias)
    p = multiscale_pool(u, pool_w, pool_scale)
    return jnp.concatenate([a, p], axis=-1) @ w_out


def setup_inputs(seed: int = 0) -> dict:
    key = jax.random.key(seed)
    ks = jax.random.split(key, 20)
    f32 = jnp.float32
    L = DEPTH

    def nrm(k, shape, fan_in):
        return jax.random.normal(k, shape, f32) * fan_in ** -0.5

    def gain(k, shape):
        return 1.0 + 0.1 * jax.random.normal(k, shape, f32)

    return {
        "x": jax.random.normal(ks[0], (BATCH, SEQ, D_MODEL), f32),
        "ffn1_norm": gain(ks[1], (L, D_MODEL)),
        "ffn1_w_gate": nrm(ks[2], (L, D_MODEL, D_FF), D_MODEL),
        "ffn1_w_up": nrm(ks[3], (L, D_MODEL, D_FF), D_MODEL),
        "ffn1_w_down": nrm(ks[4], (L, D_FF, D_MODEL), D_FF),
        "mix_norm": gain(ks[5], (L, D_MODEL)),
        "w_in": nrm(ks[6], (L, D_MODEL, D_IN), D_MODEL),
        "q_norm": gain(ks[7], (L, HEAD_DIM)),
        "k_norm": gain(ks[8], (L, HEAD_DIM)),
        "rel_bias": 0.5 * jax.random.normal(ks[9], (L, N_HEADS, N_REL), f32),
        "pool_w": nrm(ks[10], (L, N_POOL_GROUPS, POOL_GROUP_DIM, POOL_GROUP_DIM), POOL_GROUP_DIM),
        "pool_scale": gain(ks[11], (L, D_POOL)),
        "w_out": nrm(ks[12], (L, D_MIX, D_MODEL), D_MIX),
        "ffn2_norm": gain(ks[13], (L, D_MODEL)),
        "ffn2_w_gate": nrm(ks[14], (L, D_MODEL, D_FF), D_MODEL),
        "ffn2_w_up": nrm(ks[15], (L, D_MODEL, D_FF), D_MODEL),
        "ffn2_w_down": nrm(ks[16], (L, D_FF, D_MODEL), D_FF),
        "final_norm": gain(ks[17], (L, D_MODEL)),
    }


def reference(x, ffn1_norm, ffn1_w_gate, ffn1_w_up, ffn1_w_down, mix_norm, w_in,
              q_norm, k_norm, rel_bias, pool_w, pool_scale, w_out,
              ffn2_norm, ffn2_w_gate, ffn2_w_up, ffn2_w_down, final_norm):
    for l in range(DEPTH):
        x = x + 0.5 * swiglu_ffn(rmsnorm(x, ffn1_norm[l]), ffn1_w_gate[l], ffn1_w_up[l], ffn1_w_down[l])
        x = x + hybrid_mixer(rmsnorm(x, mix_norm[l]), w_in[l], q_norm[l], k_norm[l], rel_bias[l],
                             pool_w[l], pool_scale[l], w_out[l])
        x = x + 0.5 * swiglu_ffn(rmsnorm(x, ffn2_norm[l]), ffn2_w_gate[l], ffn2_w_up[l], ffn2_w_down[l])
        x = rmsnorm(x, final_norm[l])
    return x
```

```python
import functools

import jax
import jax.numpy as jnp
from jax import lax
from jax.experimental import pallas as pl
from jax.experimental.pallas import tpu as pltpu

D_MODEL = 1024
SEQ = 4096
CHUNK = 64
N_LEFT_CHUNKS = 8
D_ATTN = 512
HEAD_DIM = 64
N_HEADS = 8
D_POOL = 512
POOL_WINDOWS = (2, 4, 8, 16)
POOL_GROUP_DIM = 128
REL_CLIP = 128
D_FF = 2816
D_IN = 3 * D_ATTN + D_POOL
EPS = 1e-6
NEG_INF = -1e30

LANES = 128
MXU_DIM = 256
VMEM_LIMIT_BYTES = 56 * 1024 * 1024

TM_FFN = 512
TM_MIX = 512
TILES_PER_SEQ = SEQ // TM_MIX
Q_BLOCK = 256
WINDOW = Q_BLOCK + N_LEFT_CHUNKS * CHUNK
POOL_HALO = 16

BF16 = jnp.bfloat16
F32 = jnp.float32


def _rms(x, gain):
    return x * lax.rsqrt(jnp.mean(x * x, axis=-1, keepdims=True) + EPS) * gain


def _dot(a, b):
    return jnp.dot(a, b, preferred_element_type=F32)


def _ffn_kernel(x_ref, g_ref, wg_ref, wu_ref, wd_ref, fg_ref, o_ref, *, final_norm):
    x = x_ref[...]
    h = _rms(x, g_ref[...]).astype(BF16)
    gate = _dot(h, wg_ref[...])
    up = _dot(h, wu_ref[...])
    act = (gate * jax.nn.sigmoid(gate) * up).astype(BF16)
    y = x + 0.5 * _dot(act, wd_ref[...])
    if final_norm:
        y = _rms(y, fg_ref[...])
    o_ref[...] = y


def _resident(shape):
    return pl.BlockSpec(shape, lambda i: (0,) * len(shape), pipeline_mode=pl.Buffered(1))


def _ffn(x2d, norm_g, wg, wu, wd, final_g, *, final_norm):
    n = x2d.shape[0]
    return pl.pallas_call(
        functools.partial(_ffn_kernel, final_norm=final_norm),
        grid=(n // TM_FFN,),
        in_specs=[
            pl.BlockSpec((TM_FFN, D_MODEL), lambda i: (i, 0)),
            _resident((1, D_MODEL)),
            _resident((D_MODEL, D_FF)),
            _resident((D_MODEL, D_FF)),
            _resident((D_FF, D_MODEL)),
            _resident((1, D_MODEL)),
        ],
        out_specs=pl.BlockSpec((TM_FFN, D_MODEL), lambda i: (i, 0)),
        out_shape=jax.ShapeDtypeStruct((n, D_MODEL), F32),
        compiler_params=pltpu.CompilerParams(
            dimension_semantics=("arbitrary",), vmem_limit_bytes=VMEM_LIMIT_BYTES),
        name="ffn_final" if final_norm else "ffn",
    )(x2d, norm_g, wg, wu, wd, final_g)


def _head_rms(t, gain, bd):
    sq = t * t
    hi = sq.astype(BF16)
    lo = (sq - hi.astype(F32)).astype(BF16)
    parts = []
    for s in range(D_ATTN // MXU_DIM):
        sl = slice(s * MXU_DIM, (s + 1) * MXU_DIM)
        parts.append(_dot(hi[:, sl], bd) + _dot(lo[:, sl], bd))
    ms = jnp.concatenate(parts, axis=1)
    return t * lax.rsqrt(ms + EPS) * gain


def _mixer_kernel(x_ref, g_ref, win_ref, qg_ref, kg_ref, bd_ref, tbl_ref, pw_ref, ps_ref,
                  wout_ref, o_ref, q_scr, k_scr, v_scr, u_scr, d_scr, cat_scr):
    j = pl.program_id(0) % TILES_PER_SEQ
    first = j == 0

    @pl.when(first)
    def _():
        k_scr[0:TM_MIX, :] = jnp.zeros((TM_MIX, D_ATTN), BF16)
        v_scr[0:TM_MIX, :] = jnp.zeros((TM_MIX, D_ATTN), BF16)
        u_scr[0:POOL_HALO, :] = jnp.zeros((POOL_HALO, D_POOL), F32)

    x = x_ref[...]
    h = _rms(x, g_ref[...]).astype(BF16)
    proj = _dot(h, win_ref[...])
    bd = bd_ref[...]
    q_scr[...] = _head_rms(proj[:, 0:D_ATTN], qg_ref[...], bd).astype(BF16)
    k_scr[TM_MIX:, :] = _head_rms(proj[:, D_ATTN:2 * D_ATTN], kg_ref[...], bd).astype(BF16)
    v_scr[TM_MIX:, :] = proj[:, 2 * D_ATTN:3 * D_ATTN].astype(BF16)
    u_scr[POOL_HALO:, :] = proj[:, 3 * D_ATTN:]

    lane = lax.broadcasted_iota(jnp.int32, (Q_BLOCK, LANES), 1)
    for blk in range(TM_MIX // Q_BLOCK):
        r0 = blk * Q_BLOCK
        col = lax.broadcasted_iota(jnp.int32, (1, WINDOW), 1) + r0
        pen = jnp.where(jnp.logical_and(first, col < TM_MIX), NEG_INF, 0.0)
        for pair in range(N_HEADS // 2):
            ls = slice(pair * LANES, (pair + 1) * LANES)
            q2 = q_scr[r0:r0 + Q_BLOCK, ls]
            k2 = k_scr[r0:r0 + WINDOW, ls]
            v2 = v_scr[r0:r0 + WINDOW, ls]
            outs = []
            for hh in range(2):
                keep = (lane >= HEAD_DIM) if hh else (lane < HEAD_DIM)
                qm = jnp.where(keep, q2, jnp.zeros_like(q2))
                s = lax.dot_general(qm, k2, (((1,), (1,)), ((), ())),
                                    preferred_element_type=F32)
                s = s + tbl_ref[2 * pair + hh] + pen
                m = jnp.max(s, axis=-1, keepdims=True)
                p = jnp.exp(s - m)
                l = jnp.sum(p, axis=-1, keepdims=True)
                outs.append(_dot(p.astype(BF16), v2) / l)
            o = jnp.where(lane < HEAD_DIM, outs[0], outs[1])
            cat_scr[r0:r0 + Q_BLOCK, ls] = o.astype(BF16)

    pos = lax.broadcasted_iota(jnp.int32, (TM_MIX, 1), 0) + j * TM_MIX
    for g, w in enumerate(POOL_WINDOWS):
        ls = slice(g * POOL_GROUP_DIM, (g + 1) * POOL_GROUP_DIM)
        e = u_scr[:, ls]
        acc = e
        sh = 1
        while sh < w:
            acc = acc + pltpu.roll(acc, sh, axis=0)
            sh *= 2
        cnt = jnp.minimum(pos + 1, w).astype(F32)
        d = acc[POOL_HALO:, :] / cnt - e[POOL_HALO:, :]
        d_scr[:, ls] = d.astype(BF16)
    for pr in range(D_POOL // MXU_DIM):
        sl = slice(pr * MXU_DIM, (pr + 1) * MXU_DIM)
        y = _dot(d_scr[:, sl], pw_ref[pr]) * ps_ref[:, sl]
        cat_scr[:, D_ATTN + pr * MXU_DIM:D_ATTN + (pr + 1) * MXU_DIM] = y.astype(BF16)

    o_ref[...] = x + _dot(cat_scr[...], wout_ref[...])

    k_scr[0:TM_MIX, :] = k_scr[TM_MIX:, :]
    v_scr[0:TM_MIX, :] = v_scr[TM_MIX:, :]
    u_scr[0:POOL_HALO, :] = u_scr[TM_MIX:, :]


def _mixer(x2d, norm_g, w_in, qg, kg, bd, tbl, pw_bd, ps, w_out):
    n = x2d.shape[0]
    return pl.pallas_call(
        _mixer_kernel,
        grid=(n // TM_MIX,),
        in_specs=[
            pl.BlockSpec((TM_MIX, D_MODEL), lambda i: (i, 0)),
            _resident((1, D_MODEL)),
            _resident((D_MODEL, D_IN)),
            _resident((1, D_ATTN)),
            _resident((1, D_ATTN)),
            _resident((MXU_DIM, MXU_DIM)),
            _resident((N_HEADS, Q_BLOCK, WINDOW)),
            _resident((D_POOL // MXU_DIM, MXU_DIM, MXU_DIM)),
            _resident((1, D_POOL)),
            _resident((D_MODEL, D_MODEL)),
        ],
        out_specs=pl.BlockSpec((TM_MIX, D_MODEL), lambda i: (i, 0)),
        out_shape=jax.ShapeDtypeStruct((n, D_MODEL), F32),
        scratch_shapes=[
            pltpu.VMEM((TM_MIX, D_ATTN), BF16),
            pltpu.VMEM((2 * TM_MIX, D_ATTN), BF16),
            pltpu.VMEM((2 * TM_MIX, D_ATTN), BF16),
            pltpu.VMEM((POOL_HALO + TM_MIX, D_POOL), F32),
            pltpu.VMEM((TM_MIX, D_POOL), BF16),
            pltpu.VMEM((TM_MIX, D_MODEL), BF16),
        ],
        compiler_params=pltpu.CompilerParams(
            dimension_semantics=("arbitrary",), vmem_limit_bytes=VMEM_LIMIT_BYTES),
        name="mixer",
    )(x2d, norm_g, w_in, qg, kg, bd, tbl, pw_bd, ps, w_out)


def _bias_table(rel_bias):
    r = jnp.arange(Q_BLOCK)[:, None]
    c = jnp.arange(WINDOW)[None, :]
    band = c - (r // CHUNK) * CHUNK
    in_band = (band >= 0) & (band < (N_LEFT_CHUNKS + 1) * CHUNK)
    rel = (r % CHUNK) + N_LEFT_CHUNKS * CHUNK - band
    idx = jnp.clip(rel, -REL_CLIP, REL_CLIP) + REL_CLIP
    return jnp.where(in_band[None], rel_bias.astype(F32)[:, idx], NEG_INF)


def _block_diag2(a, b):
    z = jnp.zeros_like(a)
    return jnp.concatenate([jnp.concatenate([a, z], axis=1),
                            jnp.concatenate([z, b], axis=1)], axis=0)


def kernel(x, ffn1_norm, ffn1_w_gate, ffn1_w_up, ffn1_w_down, mix_norm, w_in, q_norm, k_norm,
           rel_bias, pool_w, pool_scale, w_out, ffn2_norm, ffn2_w_gate, ffn2_w_up, ffn2_w_down,
           final_norm):
    b, s, d = x.shape
    assert (s, d) == (SEQ, D_MODEL) and ffn1_norm.shape[0] == 1
    x2d = x.reshape(b * s, d)
    l = 0
    head_id = jnp.arange(MXU_DIM) // HEAD_DIM
    bd = jnp.where(head_id[:, None] == head_id[None, :], 1.0 / HEAD_DIM, 0.0).astype(BF16)
    qg = (jnp.tile(q_norm[l], N_HEADS) * HEAD_DIM ** -0.5)[None, :]
    kg = jnp.tile(k_norm[l], N_HEADS)[None, :]
    pw = pool_w[l].astype(BF16)
    pw_bd = jnp.stack([_block_diag2(pw[0], pw[1]), _block_diag2(pw[2], pw[3])])

    x2d = _ffn(x2d, ffn1_norm[l][None, :], ffn1_w_gate[l].astype(BF16), ffn1_w_up[l].astype(BF16),
               ffn1_w_down[l].astype(BF16), final_norm[l][None, :], final_norm=False)
    x2d = _mixer(x2d, mix_norm[l][None, :], w_in[l].astype(BF16), qg, kg, bd,
                 _bias_table(rel_bias[l]), pw_bd, pool_scale[l][None, :], w_out[l].astype(BF16))
    x2d = _ffn(x2d, ffn2_norm[l][None, :], ffn2_w_gate[l].astype(BF16), ffn2_w_up[l].astype(BF16),
               ffn2_w_down[l].astype(BF16), final_norm[l][None, :], final_norm=True)
    return x2d.reshape(b, s, d)
```

```python
import functools

import jax
import jax.numpy as jnp
from jax import lax
from jax.experimental import pallas as pl
from jax.experimental.pallas import tpu as pltpu

D_MODEL = 1024
SEQ = 4096
CHUNK = 64
N_LEFT_CHUNKS = 8
D_ATTN = 512
HEAD_DIM = 64
N_HEADS = 8
D_POOL = 512
POOL_WINDOWS = (2, 4, 8, 16)
POOL_GROUP_DIM = 128
REL_CLIP = 128
D_FF = 2816
D_IN = 3 * D_ATTN + D_POOL
EPS = 1e-6
NEG_INF = -1e30

LANES = 128
MXU_DIM = 256
VMEM_LIMIT_BYTES = 56 * 1024 * 1024

TM_FFN = 512
TM_MIX = 512
TILES_PER_SEQ = SEQ // TM_MIX
Q_BLOCK = 256
WINDOW = Q_BLOCK + N_LEFT_CHUNKS * CHUNK
POOL_HALO = 16

BF16 = jnp.bfloat16
F32 = jnp.float32


def _rms(x, gain):
    return x * lax.rsqrt(jnp.mean(x * x, axis=-1, keepdims=True) + EPS) * gain


def _dot(a, b):
    return jnp.dot(a, b, preferred_element_type=F32)


def _ffn_kernel(x_ref, g_ref, wg_ref, wu_ref, wd_ref, fg_ref, o_ref, *, final_norm):
    x = x_ref[...]
    h = _rms(x, g_ref[...]).astype(BF16)
    gate = _dot(h, wg_ref[...])
    up = _dot(h, wu_ref[...])
    act = (gate * jax.nn.sigmoid(gate) * up).astype(BF16)
    y = x + 0.5 * _dot(act, wd_ref[...])
    if final_norm:
        y = _rms(y, fg_ref[...])
    o_ref[...] = y


def _resident(shape):
    return pl.BlockSpec(shape, lambda i: (0,) * len(shape), pipeline_mode=pl.Buffered(1))


def _ffn(x2d, norm_g, wg, wu, wd, final_g, *, final_norm):
    n = x2d.shape[0]
    return pl.pallas_call(
        functools.partial(_ffn_kernel, final_norm=final_norm),
        grid=(n // TM_FFN,),
        in_specs=[
            pl.BlockSpec((TM_FFN, D_MODEL), lambda i: (i, 0)),
            _resident((1, D_MODEL)),
            _resident((D_MODEL, D_FF)),
            _resident((D_MODEL, D_FF)),
            _resident((D_FF, D_MODEL)),
            _resident((1, D_MODEL)),
        ],
        out_specs=pl.BlockSpec((TM_FFN, D_MODEL), lambda i: (i, 0)),
        out_shape=jax.ShapeDtypeStruct((n, D_MODEL), F32),
        compiler_params=pltpu.CompilerParams(
            dimension_semantics=("arbitrary",), vmem_limit_bytes=VMEM_LIMIT_BYTES),
        name="ffn_final" if final_norm else "ffn",
    )(x2d, norm_g, wg, wu, wd, final_g)


def _head_rms(t, gain, bd):
    sq = t * t
    hi = sq.astype(BF16)
    lo = (sq - hi.astype(F32)).astype(BF16)
    parts = []
    for s in range(D_ATTN // MXU_DIM):
        sl = slice(s * MXU_DIM, (s + 1) * MXU_DIM)
        parts.append(_dot(hi[:, sl], bd) + _dot(lo[:, sl], bd))
    ms = jnp.concatenate(parts, axis=1)
    return t * lax.rsqrt(ms + EPS) * gain


def _mixer_kernel(x_ref, g_ref, win_ref, qg_ref, kg_ref, bd_ref, tbl_ref, pw_ref, ps_ref,
                  wout_ref, o_ref, q_scr, k_scr, v_scr, u_scr, d_scr, cat_scr):
    j = pl.program_id(0) % TILES_PER_SEQ
    first = j == 0

    @pl.when(first)
    def _():
        k_scr[0:TM_MIX, :] = jnp.zeros((TM_MIX, D_ATTN), BF16)
        v_scr[0:TM_MIX, :] = jnp.zeros((TM_MIX, D_ATTN), BF16)
        u_scr[0:POOL_HALO, :] = jnp.zeros((POOL_HALO, D_POOL), F32)

    x = x_ref[...]
    h = _rms(x, g_ref[...]).astype(BF16)
    proj = _dot(h, win_ref[...])
    bd = bd_ref[...]
    q_scr[...] = _head_rms(proj[:, 0:D_ATTN], qg_ref[...], bd).astype(BF16)
    k_scr[TM_MIX:, :] = _head_rms(proj[:, D_ATTN:2 * D_ATTN], kg_ref[...], bd).astype(BF16)
    v_scr[TM_MIX:, :] = proj[:, 2 * D_ATTN:3 * D_ATTN].astype(BF16)
    u_scr[POOL_HALO:, :] = proj[:, 3 * D_ATTN:]

    lane = lax.broadcasted_iota(jnp.int32, (Q_BLOCK, LANES), 1)
    for blk in range(TM_MIX // Q_BLOCK):
        r0 = blk * Q_BLOCK
        col = lax.broadcasted_iota(jnp.int32, (1, WINDOW), 1) + r0
        pen = jnp.where(jnp.logical_and(first, col < TM_MIX), NEG_INF, 0.0)
        for pair in range(N_HEADS // 2):
            ls = slice(pair * LANES, (pair + 1) * LANES)
            q2 = q_scr[r0:r0 + Q_BLOCK, ls]
            k2 = k_scr[r0:r0 + WINDOW, ls]
            v2 = v_scr[r0:r0 + WINDOW, ls]
            outs = []
            for hh in range(2):
                keep = (lane >= HEAD_DIM) if hh else (lane < HEAD_DIM)
                qm = jnp.where(keep, q2, jnp.zeros_like(q2))
                s = lax.dot_general(qm, k2, (((1,), (1,)), ((), ())),
                                    preferred_element_type=F32)
                s = s + tbl_ref[2 * pair + hh] + pen
                m = jnp.max(s, axis=-1, keepdims=True)
                p = jnp.exp(s - m)
                l = jnp.sum(p, axis=-1, keepdims=True)
                outs.append(_dot(p.astype(BF16), v2) / l)
            o = jnp.where(lane < HEAD_DIM, outs[0], outs[1])
            cat_scr[r0:r0 + Q_BLOCK, ls] = o.astype(BF16)

    pos = lax.broadcasted_iota(jnp.int32, (TM_MIX, 1), 0) + j * TM_MIX
    for g, w in enumerate(POOL_WINDOWS):
        ls = slice(g * POOL_GROUP_DIM, (g + 1) * POOL_GROUP_DIM)
        e = u_scr[:, ls]
        acc = e
        sh = 1
        while sh < w:
            acc = acc + pltpu.roll(acc, sh, axis=0)
            sh *= 2
        cnt = jnp.minimum(pos + 1, w).astype(F32)
        d = acc[POOL_HALO:, :] / cnt - e[POOL_HALO:, :]
        d_scr[:, ls] = d.astype(BF16)
    for pr in range(D_POOL // MXU_DIM):
        sl = slice(pr * MXU_DIM, (pr + 1) * MXU_DIM)
        y = _dot(d_scr[:, sl], pw_ref[pr]) * ps_ref[:, sl]
        cat_scr[:, D_ATTN + pr * MXU_DIM:D_ATTN + (pr + 1) * MXU_DIM] = y.astype(BF16)

    o_ref[...] = x + _dot(cat_scr[...], wout_ref[...])

    k_scr[0:TM_MIX, :] = k_scr[TM_MIX:, :]
    v_scr[0:TM_MIX, :] = v_scr[TM_MIX:, :]
    u_scr[0:POOL_HALO, :] = u_scr[TM_MIX:, :]


def _mixer(x2d, norm_g, w_in, qg, kg, bd, tbl, pw_bd, ps, w_out):
    n = x2d.shape[0]
    return pl.pallas_call(
        _mixer_kernel,
        grid=(n // TM_MIX,),
        in_specs=[
            pl.BlockSpec((TM_MIX, D_MODEL), lambda i: (i, 0)),
            _resident((1, D_MODEL)),
            _resident((D_MODEL, D_IN)),
            _resident((1, D_ATTN)),
            _resident((1, D_ATTN)),
            _resident((MXU_DIM, MXU_DIM)),
            _resident((N_HEADS, Q_BLOCK, WINDOW)),
            _resident((D_POOL // MXU_DIM, MXU_DIM, MXU_DIM)),
            _resident((1, D_POOL)),
            _resident((D_MODEL, D_MODEL)),
        ],
        out_specs=pl.BlockSpec((TM_MIX, D_MODEL), lambda i: (i, 0)),
        out_shape=jax.ShapeDtypeStruct((n, D_MODEL), F32),
        scratch_shapes=[
            pltpu.VMEM((TM_MIX, D_ATTN), BF16),
            pltpu.VMEM((2 * TM_MIX, D_ATTN), BF16),
            pltpu.VMEM((2 * TM_MIX, D_ATTN), BF16),
            pltpu.VMEM((POOL_HALO + TM_MIX, D_POOL), F32),
            pltpu.VMEM((TM_MIX, D_POOL), BF16),
            pltpu.VMEM((TM_MIX, D_MODEL), BF16),
        ],
        compiler_params=pltpu.CompilerParams(
            dimension_semantics=("arbitrary",), vmem_limit_bytes=VMEM_LIMIT_BYTES),
        name="mixer",
    )(x2d, norm_g, w_in, qg, kg, bd, tbl, pw_bd, ps, w_out)


def _bias_table(rel_bias):
    span = N_LEFT_CHUNKS * CHUNK
    period = 1024
    far = rel_bias[:, 2 * REL_CLIP:]
    n_far = span - REL_CLIP + 1
    n_near = REL_CLIP + CHUNK - 1
    base = jnp.concatenate([
        jnp.broadcast_to(far, (N_HEADS, n_far)),
        rel_bias[:, 2 * REL_CLIP - 1:2 * REL_CLIP - 1 - n_near:-1],
        jnp.broadcast_to(far, (N_HEADS, period - n_far - n_near)),
    ], axis=1).astype(F32)
    flat = jnp.tile(base, (1, Q_BLOCK))[:, :Q_BLOCK * (period - 1)]
    toeplitz = flat.reshape(N_HEADS, Q_BLOCK, period - 1)[:, :, :WINDOW]
    r = jnp.arange(Q_BLOCK)[:, None]
    c = jnp.arange(WINDOW)[None, :]
    band = c - (r // CHUNK) * CHUNK
    in_band = (band >= 0) & (band < span + CHUNK)
    return jnp.where(in_band[None], toeplitz, NEG_INF)


def _block_diag2(a, b):
    z = jnp.zeros_like(a)
    return jnp.concatenate([jnp.concatenate([a, z], axis=1),
                            jnp.concatenate([z, b], axis=1)], axis=0)


def kernel(x, ffn1_norm, ffn1_w_gate, ffn1_w_up, ffn1_w_down, mix_norm, w_in, q_norm, k_norm,
           rel_bias, pool_w, pool_scale, w_out, ffn2_norm, ffn2_w_gate, ffn2_w_up, ffn2_w_down,
           final_norm):
    b, s, d = x.shape
    assert (s, d) == (SEQ, D_MODEL) and ffn1_norm.shape[0] == 1
    x2d = x.reshape(b * s, d)
    l = 0
    head_id = jnp.arange(MXU_DIM) // HEAD_DIM
    bd = jnp.where(head_id[:, None] == head_id[None, :], 1.0 / HEAD_DIM, 0.0).astype(BF16)
    qg = (jnp.tile(q_norm[l], N_HEADS) * HEAD_DIM ** -0.5)[None, :]
    kg = jnp.tile(k_norm[l], N_HEADS)[None, :]
    pw = pool_w[l].astype(BF16)
    pw_bd = jnp.stack([_block_diag2(pw[0], pw[1]), _block_diag2(pw[2], pw[3])])

    x2d = _ffn(x2d, ffn1_norm[l][None, :], ffn1_w_gate[l].astype(BF16), ffn1_w_up[l].astype(BF16),
               ffn1_w_down[l].astype(BF16), final_norm[l][None, :], final_norm=False)
    x2d = _mixer(x2d, mix_norm[l][None, :], w_in[l].astype(BF16), qg, kg, bd,
                 _bias_table(rel_bias[l]), pw_bd, pool_scale[l][None, :], w_out[l].astype(BF16))
    x2d = _ffn(x2d, ffn2_norm[l][None, :], ffn2_w_gate[l].astype(BF16), ffn2_w_up[l].astype(BF16),
               ffn2_w_down[l].astype(BF16), final_norm[l][None, :], final_norm=True)
    return x2d.reshape(b, s, d)
```

```python
import functools
import math

import numpy as np
import jax
import jax.numpy as jnp
from jax import lax
from jax.experimental import pallas as pl
from jax.experimental.pallas import tpu as pltpu

D_MODEL = 1024
SEQ = 4096
CHUNK = 64
N_LEFT_CHUNKS = 8
D_ATTN = 512
HEAD_DIM = 64
N_HEADS = 8
D_POOL = 512
POOL_WINDOWS = (2, 4, 8, 16)
POOL_GROUP_DIM = 128
REL_CLIP = 128
D_FF = 2816
D_IN = 3 * D_ATTN + D_POOL
EPS = 1e-6
NEG_INF = -1e30

LANES = 128
MXU_DIM = 256
VMEM_LIMIT_BYTES = 56 * 1024 * 1024

TM_FFN = 512
TM_MIX = 512
TILES_PER_SEQ = SEQ // TM_MIX
Q_BLOCK = 256
SPAN = N_LEFT_CHUNKS * CHUNK
WINDOW = Q_BLOCK + SPAN
BAND_GROUPS = (SPAN + 2 * CHUNK) // LANES
STRIP = 32
HEADS_PER_V_SLAB = MXU_DIM // HEAD_DIM
POOL_HALO = 16
LOG2E = math.log2(math.e)

BF16 = jnp.bfloat16
F32 = jnp.float32


def _rms(x, gain):
    return x * lax.rsqrt(jnp.mean(x * x, axis=-1, keepdims=True) + EPS) * gain


def _dot(a, b):
    return jnp.dot(a, b, preferred_element_type=F32)


def _ffn_kernel(x_ref, g_ref, wg_ref, wu_ref, wd_ref, fg_ref, o_ref, *, final_norm):
    x = x_ref[...]
    h = _rms(x, g_ref[...]).astype(BF16)
    gate = _dot(h, wg_ref[...])
    up = _dot(h, wu_ref[...])
    act = (gate * jax.nn.sigmoid(gate) * up).astype(BF16)
    y = x + 0.5 * _dot(act, wd_ref[...])
    if final_norm:
        y = _rms(y, fg_ref[...])
    o_ref[...] = y


def _resident(shape):
    return pl.BlockSpec(shape, lambda i: (0,) * len(shape), pipeline_mode=pl.Buffered(1))


def _ffn(x2d, norm_g, wg, wu, wd, final_g, *, final_norm):
    n = x2d.shape[0]
    return pl.pallas_call(
        functools.partial(_ffn_kernel, final_norm=final_norm),
        grid=(n // TM_FFN,),
        in_specs=[
            pl.BlockSpec((TM_FFN, D_MODEL), lambda i: (i, 0)),
            _resident((1, D_MODEL)),
            _resident((D_MODEL, D_FF)),
            _resident((D_MODEL, D_FF)),
            _resident((D_FF, D_MODEL)),
            _resident((1, D_MODEL)),
        ],
        out_specs=pl.BlockSpec((TM_FFN, D_MODEL), lambda i: (i, 0)),
        out_shape=jax.ShapeDtypeStruct((n, D_MODEL), F32),
        compiler_params=pltpu.CompilerParams(
            dimension_semantics=("arbitrary",), vmem_limit_bytes=VMEM_LIMIT_BYTES),
        name="ffn_final" if final_norm else "ffn",
    )(x2d, norm_g, wg, wu, wd, final_g)


def _head_rms(t, gain, bd):
    sq = t * t
    hi = sq.astype(BF16)
    lo = (sq - hi.astype(F32)).astype(BF16)
    parts = []
    for s in range(D_ATTN // MXU_DIM):
        sl = slice(s * MXU_DIM, (s + 1) * MXU_DIM)
        parts.append(_dot(hi[:, sl], bd) + _dot(lo[:, sl], bd))
    ms = jnp.concatenate(parts, axis=1)
    return t * lax.rsqrt(ms + EPS) * gain


def _mixer_kernel(x_ref, g_ref, win_ref, qg_ref, kg_ref, bd_ref, tbl_ref, pw_ref, ps_ref,
                  wout_ref, o_ref, q_scr, k_scr, v_scr, halo_scr, d_scr, cat_scr, p_scr):
    step = pl.program_id(0)
    j = step % TILES_PER_SEQ

    @pl.when(step == 0)
    def _():
        k_scr[...] = jnp.zeros(k_scr.shape, BF16)
        v_scr[...] = jnp.zeros(v_scr.shape, BF16)
        halo_scr[...] = jnp.zeros(halo_scr.shape, F32)
        p_scr[...] = jnp.zeros(p_scr.shape, BF16)

    parity = step % 2
    slot_rows = pl.multiple_of(parity * TM_MIX, TM_MIX)
    copy_rows = pl.multiple_of(2 * TM_MIX - parity * TM_MIX, TM_MIX)
    win_rows = pl.multiple_of(TM_MIX - parity * TM_MIX, TM_MIX)

    x = x_ref[...]
    h = _rms(x, g_ref[...]).astype(BF16)
    proj = _dot(h, win_ref[...])
    bd = bd_ref[...]
    q_scr[...] = _head_rms(proj[:, 0:D_ATTN], qg_ref[...], bd).astype(BF16)
    kn = _head_rms(proj[:, D_ATTN:2 * D_ATTN], kg_ref[...], bd).astype(BF16)
    vn = proj[:, 2 * D_ATTN:3 * D_ATTN].astype(BF16)
    k_scr[pl.ds(slot_rows, TM_MIX), :] = kn
    k_scr[pl.ds(copy_rows, TM_MIX), :] = kn
    v_scr[pl.ds(slot_rows, TM_MIX), :] = vn
    v_scr[pl.ds(copy_rows, TM_MIX), :] = vn
    u = proj[:, 3 * D_ATTN:]
    halo = jnp.where(jnp.full((POOL_HALO, D_POOL), j, jnp.int32) == 0, 0.0, halo_scr[1 - parity])
    halo_scr[parity] = u[TM_MIX - POOL_HALO:, :]

    lane = lax.broadcasted_iota(jnp.int32, (Q_BLOCK, LANES), 1)
    pen = jnp.where(jnp.full((1, LANES), j, jnp.int32) == 0, NEG_INF, 0.0)
    unit = 0
    for blk in range(TM_MIX // Q_BLOCK):
        r0 = blk * Q_BLOCK
        win = pl.ds(win_rows + r0, WINDOW)
        groups_before_seq = (TM_MIX - r0) // LANES
        for quad in range(N_HEADS // HEADS_PER_V_SLAB):
            vs = slice(quad * MXU_DIM, (quad + 1) * MXU_DIM)
            v4 = v_scr[win, vs]
            slab = [None] * (MXU_DIM // LANES)
            for hq in range(HEADS_PER_V_SLAB):
                head = quad * HEADS_PER_V_SLAB + hq
                ls = slice((head // 2) * LANES, (head // 2 + 1) * LANES)
                q2 = q_scr[r0:r0 + Q_BLOCK, ls]
                k2 = k_scr[win, ls]
                keep = (lane >= HEAD_DIM) if head % 2 else (lane < HEAD_DIM)
                qm = jnp.where(keep, q2, jnp.zeros_like(q2))
                s = lax.dot_general(qm, k2, (((1,), (1,)), ((), ())),
                                    preferred_element_type=F32)
                p_buf = p_scr.at[unit]
                unit += 1
                row_sums = []
                for rs in range(0, Q_BLOCK, STRIP):
                    cq = rs // CHUNK
                    g0 = cq // 2
                    pieces = []
                    for g in range(g0, g0 + BAND_GROUPS):
                        cs = slice(g * LANES, (g + 1) * LANES)
                        piece = s[rs:rs + STRIP, cs]
                        if _TABLE_NEEDED[cq][g]:
                            piece = piece + tbl_ref[head, rs:rs + STRIP, cs]
                        if g < groups_before_seq:
                            piece = piece + pen
                        pieces.append(piece)
                    m = jnp.max(functools.reduce(jnp.maximum, pieces), axis=-1, keepdims=True)
                    probs = [jnp.exp2(piece - m) for piece in pieces]
                    row_sums.append(
                        jnp.sum(functools.reduce(jnp.add, probs), axis=-1, keepdims=True))
                    p_buf[rs:rs + STRIP, g0 * LANES:(g0 + BAND_GROUPS) * LANES] = (
                        jnp.concatenate(probs, axis=1).astype(BF16))
                col = hq // 2
                o = _dot(p_buf[...], v4)[:, col * LANES:(col + 1) * LANES]
                o = o / jnp.concatenate(row_sums, axis=0)
                slab[col] = o if hq % 2 == 0 else jnp.where(lane < HEAD_DIM, slab[col], o)
            cat_scr[r0:r0 + Q_BLOCK, vs] = jnp.concatenate(slab, axis=1).astype(BF16)

    pos = lax.broadcasted_iota(jnp.int32, (TM_MIX, 1), 0) + j * TM_MIX
    for g, w in enumerate(POOL_WINDOWS):
        ls = slice(g * POOL_GROUP_DIM, (g + 1) * POOL_GROUP_DIM)
        e = jnp.concatenate([halo[:, ls], u[:, ls]], axis=0)
        acc = e
        sh = 1
        while sh < w:
            acc = acc + pltpu.roll(acc, sh, axis=0)
            sh *= 2
        cnt = jnp.minimum(pos + 1, w).astype(F32)
        d = acc[POOL_HALO:, :] / cnt - u[:, ls]
        d_scr[:, ls] = d.astype(BF16)
    for pr in range(D_POOL // MXU_DIM):
        sl = slice(pr * MXU_DIM, (pr + 1) * MXU_DIM)
        y = _dot(d_scr[:, sl], pw_ref[pr]) * ps_ref[:, sl]
        cat_scr[:, D_ATTN + pr * MXU_DIM:D_ATTN + (pr + 1) * MXU_DIM] = y.astype(BF16)

    o_ref[...] = x + _dot(cat_scr[...], wout_ref[...])


def _mixer(x2d, norm_g, w_in, qg, kg, bd, tbl, pw_bd, ps, w_out):
    n = x2d.shape[0]
    return pl.pallas_call(
        _mixer_kernel,
        grid=(n // TM_MIX,),
        in_specs=[
            pl.BlockSpec((TM_MIX, D_MODEL), lambda i: (i, 0)),
            _resident((1, D_MODEL)),
            _resident((D_MODEL, D_IN)),
            _resident((1, D_ATTN)),
            _resident((1, D_ATTN)),
            _resident((MXU_DIM, MXU_DIM)),
            _resident((N_HEADS, Q_BLOCK, WINDOW)),
            _resident((D_POOL // MXU_DIM, MXU_DIM, MXU_DIM)),
            _resident((1, D_POOL)),
            _resident((D_MODEL, D_MODEL)),
        ],
        out_specs=pl.BlockSpec((TM_MIX, D_MODEL), lambda i: (i, 0)),
        out_shape=jax.ShapeDtypeStruct((n, D_MODEL), F32),
        scratch_shapes=[
            pltpu.VMEM((TM_MIX, D_ATTN), BF16),
            pltpu.VMEM((3 * TM_MIX, D_ATTN), BF16),
            pltpu.VMEM((3 * TM_MIX, D_ATTN), BF16),
            pltpu.VMEM((2, POOL_HALO, D_POOL), F32),
            pltpu.VMEM((TM_MIX, D_POOL), BF16),
            pltpu.VMEM((TM_MIX, D_MODEL), BF16),
            pltpu.VMEM((N_HEADS * TM_MIX // Q_BLOCK, Q_BLOCK, WINDOW), BF16),
        ],
        compiler_params=pltpu.CompilerParams(
            dimension_semantics=("arbitrary",), vmem_limit_bytes=VMEM_LIMIT_BYTES),
        name="mixer",
    )(x2d, norm_g, w_in, qg, kg, bd, tbl, pw_bd, ps, w_out)


def _band_masks():
    r = np.arange(Q_BLOCK)[:, None]
    c = np.arange(WINDOW)[None, :]
    band = c - (r // CHUNK) * CHUNK
    in_band = (band >= 0) & (band < SPAN + CHUNK)
    far = in_band & (c - r <= SPAN - REL_CLIP)
    return in_band, far


def _table_needed():
    in_band, far = _band_masks()
    needed = []
    for cq in range(Q_BLOCK // CHUNK):
        rows = slice(cq * CHUNK, (cq + 1) * CHUNK)
        row = []
        for g in range(WINDOW // LANES):
            cols = slice(g * LANES, (g + 1) * LANES)
            inside = cq // 2 <= g < cq // 2 + BAND_GROUPS
            assert inside or not in_band[rows, cols].any()
            row.append(bool(inside and not far[rows, cols].all()))
        needed.append(row)
    return needed


_TABLE_NEEDED = _table_needed()


def _bias_table(rel_bias):
    period = 1024
    n_far = SPAN - REL_CLIP + 1
    n_near = REL_CLIP + CHUNK - 1
    rel = (rel_bias - rel_bias[:, 2 * REL_CLIP:]).astype(F32) * LOG2E
    base = jnp.concatenate([
        jnp.zeros((N_HEADS, n_far), F32),
        rel[:, 2 * REL_CLIP - 1:2 * REL_CLIP - 1 - n_near:-1],
        jnp.zeros((N_HEADS, period - n_far - n_near), F32),
    ], axis=1)
    flat = jnp.tile(base, (1, Q_BLOCK))[:, :Q_BLOCK * (period - 1)]
    toeplitz = flat.reshape(N_HEADS, Q_BLOCK, period - 1)[:, :, :WINDOW]
    in_band, _ = _band_masks()
    return jnp.where(in_band[None], toeplitz, NEG_INF)


def _block_diag2(a, b):
    z = jnp.zeros_like(a)
    return jnp.concatenate([jnp.concatenate([a, z], axis=1),
                            jnp.concatenate([z, b], axis=1)], axis=0)


def kernel(x, ffn1_norm, ffn1_w_gate, ffn1_w_up, ffn1_w_down, mix_norm, w_in, q_norm, k_norm,
           rel_bias, pool_w, pool_scale, w_out, ffn2_norm, ffn2_w_gate, ffn2_w_up, ffn2_w_down,
           final_norm):
    b, s, d = x.shape
    assert (s, d) == (SEQ, D_MODEL) and ffn1_norm.shape[0] == 1
    x2d = x.reshape(b * s, d)
    l = 0
    head_id = jnp.arange(MXU_DIM) // HEAD_DIM
    bd = jnp.where(head_id[:, None] == head_id[None, :], 1.0 / HEAD_DIM, 0.0).astype(BF16)
    qg = (jnp.tile(q_norm[l], N_HEADS) * (HEAD_DIM ** -0.5 * LOG2E))[None, :]
    kg = jnp.tile(k_norm[l], N_HEADS)[None, :]
    pw = pool_w[l].astype(BF16)
    pw_bd = jnp.stack([_block_diag2(pw[0], pw[1]), _block_diag2(pw[2], pw[3])])

    x2d = _ffn(x2d, ffn1_norm[l][None, :], ffn1_w_gate[l].astype(BF16), ffn1_w_up[l].astype(BF16),
               ffn1_w_down[l].astype(BF16), final_norm[l][None, :], final_norm=False)
    x2d = _mixer(x2d, mix_norm[l][None, :], w_in[l].astype(BF16), qg, kg, bd,
                 _bias_table(rel_bias[l]), pw_bd, pool_scale[l][None, :], w_out[l].astype(BF16))
    x2d = _ffn(x2d, ffn2_norm[l][None, :], ffn2_w_gate[l].astype(BF16), ffn2_w_up[l].astype(BF16),
               ffn2_w_down[l].astype(BF16), final_norm[l][None, :], final_norm=True)
    return x2d.reshape(b, s, d)
```

```python
import functools
import math

import numpy as np
import jax
import jax.numpy as jnp
from jax import lax
from jax.experimental import pallas as pl
from jax.experimental.pallas import tpu as pltpu

D_MODEL = 1024
SEQ = 4096
CHUNK = 64
N_LEFT_CHUNKS = 8
D_ATTN = 512
HEAD_DIM = 64
N_HEADS = 8
D_POOL = 512
POOL_WINDOWS = (2, 4, 8, 16)
POOL_GROUP_DIM = 128
REL_CLIP = 128
D_FF = 2816
D_IN = 3 * D_ATTN + D_POOL
EPS = 1e-6
NEG_INF = -1e30

LANES = 128
MXU_DIM = 256
VMEM_LIMIT_BYTES = 56 * 1024 * 1024

TM_FFN = 512
TM_MIX = 512
TILES_PER_SEQ = SEQ // TM_MIX
Q_BLOCK = 256
SPAN = N_LEFT_CHUNKS * CHUNK
WINDOW = Q_BLOCK + SPAN
BAND_GROUPS = (SPAN + 2 * CHUNK) // LANES
STRIP = 32
HEADS_PER_V_SLAB = MXU_DIM // HEAD_DIM
POOL_HALO = 16
LOG2E = math.log2(math.e)

BF16 = jnp.bfloat16
F32 = jnp.float32


def _rms(x, gain):
    return x * lax.rsqrt(jnp.mean(x * x, axis=-1, keepdims=True) + EPS) * gain


def _dot(a, b):
    return jnp.dot(a, b, preferred_element_type=F32)


def _ffn_kernel(x_ref, g_ref, wg_ref, wu_ref, wd_ref, fg_ref, o_ref, *, final_norm):
    x = x_ref[...]
    h = _rms(x, g_ref[...]).astype(BF16)
    gate = _dot(h, wg_ref[...])
    up = _dot(h, wu_ref[...])
    act = (gate * jax.nn.sigmoid(gate) * up).astype(BF16)
    y = x + 0.5 * _dot(act, wd_ref[...])
    if final_norm:
        y = _rms(y, fg_ref[...])
    o_ref[...] = y


def _resident(shape):
    return pl.BlockSpec(shape, lambda i: (0,) * len(shape), pipeline_mode=pl.Buffered(1))


def _ffn(x2d, norm_g, wg, wu, wd, final_g, *, final_norm):
    n = x2d.shape[0]
    return pl.pallas_call(
        functools.partial(_ffn_kernel, final_norm=final_norm),
        grid=(n // TM_FFN,),
        in_specs=[
            pl.BlockSpec((TM_FFN, D_MODEL), lambda i: (i, 0)),
            _resident((1, D_MODEL)),
            _resident((D_MODEL, D_FF)),
            _resident((D_MODEL, D_FF)),
            _resident((D_FF, D_MODEL)),
            _resident((1, D_MODEL)),
        ],
        out_specs=pl.BlockSpec((TM_FFN, D_MODEL), lambda i: (i, 0)),
        out_shape=jax.ShapeDtypeStruct((n, D_MODEL), F32),
        compiler_params=pltpu.CompilerParams(
            dimension_semantics=("arbitrary",), vmem_limit_bytes=VMEM_LIMIT_BYTES),
        name="ffn_final" if final_norm else "ffn",
    )(x2d, norm_g, wg, wu, wd, final_g)


def _head_rms(t, gain, bd):
    sq = t * t
    hi = sq.astype(BF16)
    lo = (sq - hi.astype(F32)).astype(BF16)
    parts = []
    for s in range(D_ATTN // MXU_DIM):
        sl = slice(s * MXU_DIM, (s + 1) * MXU_DIM)
        parts.append(_dot(hi[:, sl], bd) + _dot(lo[:, sl], bd))
    ms = jnp.concatenate(parts, axis=1)
    return t * lax.rsqrt(ms + EPS) * gain


def _mixer_kernel(x_ref, g_ref, win_ref, qg_ref, kg_ref, bd_ref, tbl_ref, pw_ref, ps_ref,
                  wout_ref, o_ref, q_scr, kt_scr, v_scr, halo_scr, d_scr, cat_scr, p_scr):
    step = pl.program_id(0)
    j = step % TILES_PER_SEQ

    @pl.when(step == 0)
    def _():
        kt_scr[...] = jnp.zeros(kt_scr.shape, BF16)
        v_scr[...] = jnp.zeros(v_scr.shape, BF16)
        halo_scr[...] = jnp.zeros(halo_scr.shape, F32)
        p_scr[...] = jnp.zeros(p_scr.shape, BF16)

    parity = step % 2
    slot_rows = pl.multiple_of(parity * TM_MIX, TM_MIX)
    copy_rows = pl.multiple_of(2 * TM_MIX - parity * TM_MIX, TM_MIX)
    win_rows = pl.multiple_of(TM_MIX - parity * TM_MIX, TM_MIX)

    x = x_ref[...]
    h = _rms(x, g_ref[...]).astype(BF16)
    proj = _dot(h, win_ref[...])
    bd = bd_ref[...]
    q_scr[...] = _head_rms(proj[:, 0:D_ATTN], qg_ref[...], bd).astype(BF16)
    kn = _head_rms(proj[:, D_ATTN:2 * D_ATTN], kg_ref[...], bd)
    kt_scr[parity] = kn.T.astype(BF16)
    vn = proj[:, 2 * D_ATTN:3 * D_ATTN].astype(BF16)
    v_scr[pl.ds(slot_rows, TM_MIX), :] = vn
    v_scr[pl.ds(copy_rows, TM_MIX), :] = vn
    u = proj[:, 3 * D_ATTN:]
    halo = jnp.where(jnp.full((POOL_HALO, D_POOL), j, jnp.int32) == 0, 0.0, halo_scr[1 - parity])
    halo_scr[parity] = u[TM_MIX - POOL_HALO:, :]

    lane = lax.broadcasted_iota(jnp.int32, (Q_BLOCK, LANES), 1)
    pen = jnp.where(jnp.full((1, LANES), j, jnp.int32) == 0, NEG_INF, 0.0)
    unit = 0
    for blk in range(TM_MIX // Q_BLOCK):
        r0 = blk * Q_BLOCK
        win = pl.ds(win_rows + r0, WINDOW)
        n_prev = TM_MIX - r0
        groups_before_seq = n_prev // LANES
        for quad in range(N_HEADS // HEADS_PER_V_SLAB):
            vs = slice(quad * MXU_DIM, (quad + 1) * MXU_DIM)
            v4 = v_scr[win, vs]
            slab = [None] * (MXU_DIM // LANES)
            for hq in range(HEADS_PER_V_SLAB):
                head = quad * HEADS_PER_V_SLAB + hq
                ls = slice((head // 2) * LANES, (head // 2 + 1) * LANES)
                q2 = q_scr[r0:r0 + Q_BLOCK, ls]
                keep = (lane >= HEAD_DIM) if head % 2 else (lane < HEAD_DIM)
                qm = jnp.where(keep, q2, jnp.zeros_like(q2))
                s_prev = _dot(qm, kt_scr[1 - parity, ls, TM_MIX - n_prev:])
                s_cur = _dot(qm, kt_scr[parity, ls, 0:WINDOW - n_prev])
                p_buf = p_scr.at[unit]
                unit += 1
                row_sums = []
                for rs in range(0, Q_BLOCK, STRIP):
                    cq = rs // CHUNK
                    g0 = cq // 2
                    pieces = []
                    for g in range(g0, g0 + BAND_GROUPS):
                        cs = slice(g * LANES, (g + 1) * LANES)
                        if g * LANES < n_prev:
                            piece = s_prev[rs:rs + STRIP, cs]
                        else:
                            piece = s_cur[rs:rs + STRIP, g * LANES - n_prev:(g + 1) * LANES - n_prev]
                        if _TABLE_NEEDED[cq][g]:
                            piece = piece + tbl_ref[head, rs:rs + STRIP, cs]
                        if g < groups_before_seq:
                            piece = piece + pen
                        pieces.append(piece)
                    m = jnp.max(functools.reduce(jnp.maximum, pieces), axis=-1, keepdims=True)
                    probs = [jnp.exp2(piece - m) for piece in pieces]
                    row_sums.append(
                        jnp.sum(functools.reduce(jnp.add, probs), axis=-1, keepdims=True))
                    p_buf[rs:rs + STRIP, g0 * LANES:(g0 + BAND_GROUPS) * LANES] = (
                        jnp.concatenate(probs, axis=1).astype(BF16))
                col = hq // 2
                o = _dot(p_buf[...], v4)[:, col * LANES:(col + 1) * LANES]
                o = o / jnp.concatenate(row_sums, axis=0)
                slab[col] = o if hq % 2 == 0 else jnp.where(lane < HEAD_DIM, slab[col], o)
            cat_scr[r0:r0 + Q_BLOCK, vs] = jnp.concatenate(slab, axis=1).astype(BF16)

    pos = lax.broadcasted_iota(jnp.int32, (TM_MIX, 1), 0) + j * TM_MIX
    for g, w in enumerate(POOL_WINDOWS):
        ls = slice(g * POOL_GROUP_DIM, (g + 1) * POOL_GROUP_DIM)
        e = jnp.concatenate([halo[:, ls], u[:, ls]], axis=0)
        acc = e
        sh = 1
        while sh < w:
            acc = acc + pltpu.roll(acc, sh, axis=0)
            sh *= 2
        cnt = jnp.minimum(pos + 1, w).astype(F32)
        d = acc[POOL_HALO:, :] / cnt - u[:, ls]
        d_scr[:, ls] = d.astype(BF16)
    for pr in range(D_POOL // MXU_DIM):
        sl = slice(pr * MXU_DIM, (pr + 1) * MXU_DIM)
        y = _dot(d_scr[:, sl], pw_ref[pr]) * ps_ref[:, sl]
        cat_scr[:, D_ATTN + pr * MXU_DIM:D_ATTN + (pr + 1) * MXU_DIM] = y.astype(BF16)

    o_ref[...] = x + _dot(cat_scr[...], wout_ref[...])


def _mixer(x2d, norm_g, w_in, qg, kg, bd, tbl, pw_bd, ps, w_out):
    n = x2d.shape[0]
    return pl.pallas_call(
        _mixer_kernel,
        grid=(n // TM_MIX,),
        in_specs=[
            pl.BlockSpec((TM_MIX, D_MODEL), lambda i: (i, 0)),
            _resident((1, D_MODEL)),
            _resident((D_MODEL, D_IN)),
            _resident((1, D_ATTN)),
            _resident((1, D_ATTN)),
            _resident((MXU_DIM, MXU_DIM)),
            _resident((N_HEADS, Q_BLOCK, WINDOW)),
            _resident((D_POOL // MXU_DIM, MXU_DIM, MXU_DIM)),
            _resident((1, D_POOL)),
            _resident((D_MODEL, D_MODEL)),
        ],
        out_specs=pl.BlockSpec((TM_MIX, D_MODEL), lambda i: (i, 0)),
        out_shape=jax.ShapeDtypeStruct((n, D_MODEL), F32),
        scratch_shapes=[
            pltpu.VMEM((TM_MIX, D_ATTN), BF16),
            pltpu.VMEM((2, D_ATTN, TM_MIX), BF16),
            pltpu.VMEM((3 * TM_MIX, D_ATTN), BF16),
            pltpu.VMEM((2, POOL_HALO, D_POOL), F32),
            pltpu.VMEM((TM_MIX, D_POOL), BF16),
            pltpu.VMEM((TM_MIX, D_MODEL), BF16),
            pltpu.VMEM((N_HEADS * TM_MIX // Q_BLOCK, Q_BLOCK, WINDOW), BF16),
        ],
        compiler_params=pltpu.CompilerParams(
            dimension_semantics=("arbitrary",), vmem_limit_bytes=VMEM_LIMIT_BYTES),
        name="mixer",
    )(x2d, norm_g, w_in, qg, kg, bd, tbl, pw_bd, ps, w_out)


def _band_masks():
    r = np.arange(Q_BLOCK)[:, None]
    c = np.arange(WINDOW)[None, :]
    band = c - (r // CHUNK) * CHUNK
    in_band = (band >= 0) & (band < SPAN + CHUNK)
    far = in_band & (c - r <= SPAN - REL_CLIP)
    return in_band, far


def _table_needed():
    in_band, far = _band_masks()
    needed = []
    for cq in range(Q_BLOCK // CHUNK):
        rows = slice(cq * CHUNK, (cq + 1) * CHUNK)
        row = []
        for g in range(WINDOW // LANES):
            cols = slice(g * LANES, (g + 1) * LANES)
            inside = cq // 2 <= g < cq // 2 + BAND_GROUPS
            assert inside or not in_band[rows, cols].any()
            row.append(bool(inside and not far[rows, cols].all()))
        needed.append(row)
    return needed


_TABLE_NEEDED = _table_needed()


def _bias_table(rel_bias):
    period = 1024
    n_far = SPAN - REL_CLIP + 1
    n_near = REL_CLIP + CHUNK - 1
    rel = (rel_bias - rel_bias[:, 2 * REL_CLIP:]).astype(F32) * LOG2E
    base = jnp.concatenate([
        jnp.zeros((N_HEADS, n_far), F32),
        rel[:, 2 * REL_CLIP - 1:2 * REL_CLIP - 1 - n_near:-1],
        jnp.zeros((N_HEADS, period - n_far - n_near), F32),
    ], axis=1)
    flat = jnp.tile(base, (1, Q_BLOCK))[:, :Q_BLOCK * (period - 1)]
    toeplitz = flat.reshape(N_HEADS, Q_BLOCK, period - 1)[:, :, :WINDOW]
    in_band, _ = _band_masks()
    return jnp.where(in_band[None], toeplitz, NEG_INF)


def _block_diag2(a, b):
    z = jnp.zeros_like(a)
    return jnp.concatenate([jnp.concatenate([a, z], axis=1),
                            jnp.concatenate([z, b], axis=1)], axis=0)


def kernel(x, ffn1_norm, ffn1_w_gate, ffn1_w_up, ffn1_w_down, mix_norm, w_in, q_norm, k_norm,
           rel_bias, pool_w, pool_scale, w_out, ffn2_norm, ffn2_w_gate, ffn2_w_up, ffn2_w_down,
           final_norm):
    b, s, d = x.shape
    assert (s, d) == (SEQ, D_MODEL) and ffn1_norm.shape[0] == 1
    x2d = x.reshape(b * s, d)
    l = 0
    head_id = jnp.arange(MXU_DIM) // HEAD_DIM
    bd = jnp.where(head_id[:, None] == head_id[None, :], 1.0 / HEAD_DIM, 0.0).astype(BF16)
    qg = (jnp.tile(q_norm[l], N_HEADS) * (HEAD_DIM ** -0.5 * LOG2E))[None, :]
    kg = jnp.tile(k_norm[l], N_HEADS)[None, :]
    pw = pool_w[l].astype(BF16)
    pw_bd = jnp.stack([_block_diag2(pw[0], pw[1]), _block_diag2(pw[2], pw[3])])

    x2d = _ffn(x2d, ffn1_norm[l][None, :], ffn1_w_gate[l].astype(BF16), ffn1_w_up[l].astype(BF16),
               ffn1_w_down[l].astype(BF16), final_norm[l][None, :], final_norm=False)
    x2d = _mixer(x2d, mix_norm[l][None, :], w_in[l].astype(BF16), qg, kg, bd,
                 _bias_table(rel_bias[l]), pw_bd, pool_scale[l][None, :], w_out[l].astype(BF16))
    x2d = _ffn(x2d, ffn2_norm[l][None, :], ffn2_w_gate[l].astype(BF16), ffn2_w_up[l].astype(BF16),
               ffn2_w_down[l].astype(BF16), final_norm[l][None, :], final_norm=True)
    return x2d.reshape(b, s, d)
```

```python
import functools
import math

import numpy as np
import jax
import jax.numpy as jnp
from jax import lax
from jax.experimental import pallas as pl
from jax.experimental.pallas import tpu as pltpu

D_MODEL = 1024
SEQ = 4096
CHUNK = 64
N_LEFT_CHUNKS = 8
D_ATTN = 512
HEAD_DIM = 64
N_HEADS = 8
D_POOL = 512
POOL_WINDOWS = (2, 4, 8, 16)
POOL_GROUP_DIM = 128
REL_CLIP = 128
D_FF = 2816
D_IN = 3 * D_ATTN + D_POOL
EPS = 1e-6
NEG_INF = -1e30

LANES = 128
MXU_DIM = 256
VMEM_LIMIT_BYTES = 56 * 1024 * 1024

TM_FFN = 1024
TM_MIX = 512
TILES_PER_SEQ = SEQ // TM_MIX
Q_BLOCK = 256
SPAN = N_LEFT_CHUNKS * CHUNK
WINDOW = Q_BLOCK + SPAN
BAND_GROUPS = (SPAN + 2 * CHUNK) // LANES
STRIP = 32
HEADS_PER_V_SLAB = MXU_DIM // HEAD_DIM
POOL_HALO = 16
LOG2E = math.log2(math.e)

BF16 = jnp.bfloat16
F32 = jnp.float32


def _rms(x, gain):
    return x * lax.rsqrt(jnp.mean(x * x, axis=-1, keepdims=True) + EPS) * gain


def _dot(a, b):
    return jnp.dot(a, b, preferred_element_type=F32)


def _ffn_kernel(x_ref, g_ref, wg_ref, wu_ref, wd_ref, fg_ref, o_ref, *, final_norm):
    x = x_ref[...]
    h = _rms(x, g_ref[...]).astype(BF16)
    gate = _dot(h, wg_ref[...])
    up = _dot(h, wu_ref[...])
    act = (gate * jax.nn.sigmoid(gate) * up).astype(BF16)
    y = x + 0.5 * _dot(act, wd_ref[...])
    if final_norm:
        y = _rms(y, fg_ref[...])
    o_ref[...] = y


def _resident(shape):
    return pl.BlockSpec(shape, lambda i: (0,) * len(shape), pipeline_mode=pl.Buffered(1))


def _ffn(x2d, norm_g, wg, wu, wd, final_g, *, final_norm):
    n = x2d.shape[0]
    return pl.pallas_call(
        functools.partial(_ffn_kernel, final_norm=final_norm),
        grid=(n // TM_FFN,),
        in_specs=[
            pl.BlockSpec((TM_FFN, D_MODEL), lambda i: (i, 0)),
            _resident((1, D_MODEL)),
            _resident((D_MODEL, D_FF)),
            _resident((D_MODEL, D_FF)),
            _resident((D_FF, D_MODEL)),
            _resident((1, D_MODEL)),
        ],
        out_specs=pl.BlockSpec((TM_FFN, D_MODEL), lambda i: (i, 0)),
        out_shape=jax.ShapeDtypeStruct((n, D_MODEL), F32),
        compiler_params=pltpu.CompilerParams(
            dimension_semantics=("arbitrary",), vmem_limit_bytes=VMEM_LIMIT_BYTES),
        name="ffn_final" if final_norm else "ffn",
    )(x2d, norm_g, wg, wu, wd, final_g)


def _head_rms(t, gain, bd):
    sq = t * t
    hi = sq.astype(BF16)
    lo = (sq - hi.astype(F32)).astype(BF16)
    parts = []
    for s in range(D_ATTN // MXU_DIM):
        sl = slice(s * MXU_DIM, (s + 1) * MXU_DIM)
        parts.append(_dot(hi[:, sl], bd) + _dot(lo[:, sl], bd))
    ms = jnp.concatenate(parts, axis=1)
    return t * lax.rsqrt(ms + EPS) * gain


def _mixer_kernel(x_ref, g_ref, win_ref, qg_ref, kg_ref, bd_ref, tbl_ref, pw_ref, ps_ref,
                  wout_ref, o_ref, q_scr, k_scr, v_scr, halo_scr, d_scr, cat_scr, p_scr):
    step = pl.program_id(0)
    j = step % TILES_PER_SEQ

    @pl.when(step == 0)
    def _():
        k_scr[...] = jnp.zeros(k_scr.shape, BF16)
        v_scr[...] = jnp.zeros(v_scr.shape, BF16)
        halo_scr[...] = jnp.zeros(halo_scr.shape, F32)
        p_scr[...] = jnp.zeros(p_scr.shape, BF16)

    parity = step % 2
    slot_rows = pl.multiple_of(parity * TM_MIX, TM_MIX)
    copy_rows = pl.multiple_of(2 * TM_MIX - parity * TM_MIX, TM_MIX)
    win_rows = pl.multiple_of(TM_MIX - parity * TM_MIX, TM_MIX)

    x = x_ref[...]
    h = _rms(x, g_ref[...]).astype(BF16)
    proj = _dot(h, win_ref[...])
    bd = bd_ref[...]
    q_scr[...] = _head_rms(proj[:, 0:D_ATTN], qg_ref[...], bd).astype(BF16)
    kn = _head_rms(proj[:, D_ATTN:2 * D_ATTN], kg_ref[...], bd).astype(BF16)
    vn = proj[:, 2 * D_ATTN:3 * D_ATTN].astype(BF16)
    k_scr[pl.ds(slot_rows, TM_MIX), :] = kn
    k_scr[pl.ds(copy_rows, TM_MIX), :] = kn
    v_scr[pl.ds(slot_rows, TM_MIX), :] = vn
    v_scr[pl.ds(copy_rows, TM_MIX), :] = vn
    u = proj[:, 3 * D_ATTN:]
    halo = jnp.where(jnp.full((POOL_HALO, D_POOL), j, jnp.int32) == 0, 0.0, halo_scr[1 - parity])
    halo_scr[parity] = u[TM_MIX - POOL_HALO:, :]

    lane = lax.broadcasted_iota(jnp.int32, (Q_BLOCK, LANES), 1)
    pen = jnp.where(jnp.full((1, LANES), j, jnp.int32) == 0, NEG_INF, 0.0)
    unit = 0
    for blk in range(TM_MIX // Q_BLOCK):
        r0 = blk * Q_BLOCK
        win = pl.ds(win_rows + r0, WINDOW)
        groups_before_seq = (TM_MIX - r0) // LANES
        for quad in range(N_HEADS // HEADS_PER_V_SLAB):
            vs = slice(quad * MXU_DIM, (quad + 1) * MXU_DIM)
            v4 = v_scr[win, vs]
            slab = [None] * (MXU_DIM // LANES)
            for hq in range(HEADS_PER_V_SLAB):
                head = quad * HEADS_PER_V_SLAB + hq
                ls = slice((head // 2) * LANES, (head // 2 + 1) * LANES)
                q2 = q_scr[r0:r0 + Q_BLOCK, ls]
                k2 = k_scr[win, ls]
                keep = (lane >= HEAD_DIM) if head % 2 else (lane < HEAD_DIM)
                qm = jnp.where(keep, q2, jnp.zeros_like(q2))
                s = lax.dot_general(qm, k2, (((1,), (1,)), ((), ())),
                                    preferred_element_type=F32)
                p_buf = p_scr.at[unit]
                unit += 1
                row_sums = []
                for rs in range(0, Q_BLOCK, STRIP):
                    cq = rs // CHUNK
                    g0 = cq // 2
                    pieces = []
                    for g in range(g0, g0 + BAND_GROUPS):
                        cs = slice(g * LANES, (g + 1) * LANES)
                        piece = s[rs:rs + STRIP, cs]
                        if _TABLE_NEEDED[cq][g]:
                            piece = piece + tbl_ref[head, rs:rs + STRIP, cs]
                        if g < groups_before_seq:
                            piece = piece + pen
                        pieces.append(piece)
                    m = jnp.max(functools.reduce(jnp.maximum, pieces), axis=-1, keepdims=True)
                    probs = [jnp.exp2(piece - m) for piece in pieces]
                    row_sums.append(
                        jnp.sum(functools.reduce(jnp.add, probs), axis=-1, keepdims=True))
                    p_buf[rs:rs + STRIP, g0 * LANES:(g0 + BAND_GROUPS) * LANES] = (
                        jnp.concatenate(probs, axis=1).astype(BF16))
                col = hq // 2
                o = _dot(p_buf[...], v4)[:, col * LANES:(col + 1) * LANES]
                o = o / jnp.concatenate(row_sums, axis=0)
                slab[col] = o if hq % 2 == 0 else jnp.where(lane < HEAD_DIM, slab[col], o)
            cat_scr[r0:r0 + Q_BLOCK, vs] = jnp.concatenate(slab, axis=1).astype(BF16)

    pos = lax.broadcasted_iota(jnp.int32, (TM_MIX, 1), 0) + j * TM_MIX
    for g, w in enumerate(POOL_WINDOWS):
        ls = slice(g * POOL_GROUP_DIM, (g + 1) * POOL_GROUP_DIM)
        e = jnp.concatenate([halo[:, ls], u[:, ls]], axis=0)
        acc = e
        sh = 1
        while sh < w:
            acc = acc + pltpu.roll(acc, sh, axis=0)
            sh *= 2
        cnt = jnp.minimum(pos + 1, w).astype(F32)
        d = acc[POOL_HALO:, :] / cnt - u[:, ls]
        d_scr[:, ls] = d.astype(BF16)
    for pr in range(D_POOL // MXU_DIM):
        sl = slice(pr * MXU_DIM, (pr + 1) * MXU_DIM)
        y = _dot(d_scr[:, sl], pw_ref[pr]) * ps_ref[:, sl]
        cat_scr[:, D_ATTN + pr * MXU_DIM:D_ATTN + (pr + 1) * MXU_DIM] = y.astype(BF16)

    o_ref[...] = x + _dot(cat_scr[...], wout_ref[...])


def _mixer(x2d, norm_g, w_in, qg, kg, bd, tbl, pw_bd, ps, w_out):
    n = x2d.shape[0]
    return pl.pallas_call(
        _mixer_kernel,
        grid=(n // TM_MIX,),
        in_specs=[
            pl.BlockSpec((TM_MIX, D_MODEL), lambda i: (i, 0)),
            _resident((1, D_MODEL)),
            _resident((D_MODEL, D_IN)),
            _resident((1, D_ATTN)),
            _resident((1, D_ATTN)),
            _resident((MXU_DIM, MXU_DIM)),
            _resident((N_HEADS, Q_BLOCK, WINDOW)),
            _resident((D_POOL // MXU_DIM, MXU_DIM, MXU_DIM)),
            _resident((1, D_POOL)),
            _resident((D_MODEL, D_MODEL)),
        ],
        out_specs=pl.BlockSpec((TM_MIX, D_MODEL), lambda i: (i, 0)),
        out_shape=jax.ShapeDtypeStruct((n, D_MODEL), F32),
        scratch_shapes=[
            pltpu.VMEM((TM_MIX, D_ATTN), BF16),
            pltpu.VMEM((3 * TM_MIX, D_ATTN), BF16),
            pltpu.VMEM((3 * TM_MIX, D_ATTN), BF16),
            pltpu.VMEM((2, POOL_HALO, D_POOL), F32),
            pltpu.VMEM((TM_MIX, D_POOL), BF16),
            pltpu.VMEM((TM_MIX, D_MODEL), BF16),
            pltpu.VMEM((N_HEADS * TM_MIX // Q_BLOCK, Q_BLOCK, WINDOW), BF16),
        ],
        compiler_params=pltpu.CompilerParams(
            dimension_semantics=("arbitrary",), vmem_limit_bytes=VMEM_LIMIT_BYTES),
        name="mixer",
    )(x2d, norm_g, w_in, qg, kg, bd, tbl, pw_bd, ps, w_out)


def _band_masks():
    r = np.arange(Q_BLOCK)[:, None]
    c = np.arange(WINDOW)[None, :]
    band = c - (r // CHUNK) * CHUNK
    in_band = (band >= 0) & (band < SPAN + CHUNK)
    far = in_band & (c - r <= SPAN - REL_CLIP)
    return in_band, far


def _table_needed():
    in_band, far = _band_masks()
    needed = []
    for cq in range(Q_BLOCK // CHUNK):
        rows = slice(cq * CHUNK, (cq + 1) * CHUNK)
        row = []
        for g in range(WINDOW // LANES):
            cols = slice(g * LANES, (g + 1) * LANES)
            inside = cq // 2 <= g < cq // 2 + BAND_GROUPS
            assert inside or not in_band[rows, cols].any()
            row.append(bool(inside and not far[rows, cols].all()))
        needed.append(row)
    return needed


_TABLE_NEEDED = _table_needed()


def _bias_table(rel_bias):
    period = 1024
    n_far = SPAN - REL_CLIP + 1
    n_near = REL_CLIP + CHUNK - 1
    rel = (rel_bias - rel_bias[:, 2 * REL_CLIP:]).astype(F32) * LOG2E
    base = jnp.concatenate([
        jnp.zeros((N_HEADS, n_far), F32),
        rel[:, 2 * REL_CLIP - 1:2 * REL_CLIP - 1 - n_near:-1],
        jnp.zeros((N_HEADS, period - n_far - n_near), F32),
    ], axis=1)
    flat = jnp.tile(base, (1, Q_BLOCK))[:, :Q_BLOCK * (period - 1)]
    toeplitz = flat.reshape(N_HEADS, Q_BLOCK, period - 1)[:, :, :WINDOW]
    in_band, _ = _band_masks()
    return jnp.where(in_band[None], toeplitz, NEG_INF)


def _block_diag2(a, b):
    z = jnp.zeros_like(a)
    return jnp.concatenate([jnp.concatenate([a, z], axis=1),
                            jnp.concatenate([z, b], axis=1)], axis=0)


def kernel(x, ffn1_norm, ffn1_w_gate, ffn1_w_up, ffn1_w_down, mix_norm, w_in, q_norm, k_norm,
           rel_bias, pool_w, pool_scale, w_out, ffn2_norm, ffn2_w_gate, ffn2_w_up, ffn2_w_down,
           final_norm):
    b, s, d = x.shape
    assert (s, d) == (SEQ, D_MODEL) and ffn1_norm.shape[0] == 1
    x2d = x.reshape(b * s, d)
    l = 0
    head_id = jnp.arange(MXU_DIM) // HEAD_DIM
    bd = jnp.where(head_id[:, None] == head_id[None, :], 1.0 / HEAD_DIM, 0.0).astype(BF16)
    qg = (jnp.tile(q_norm[l], N_HEADS) * (HEAD_DIM ** -0.5 * LOG2E))[None, :]
    kg = jnp.tile(k_norm[l], N_HEADS)[None, :]
    pw = pool_w[l].astype(BF16)
    pw_bd = jnp.stack([_block_diag2(pw[0], pw[1]), _block_diag2(pw[2], pw[3])])

    x2d = _ffn(x2d, ffn1_norm[l][None, :], ffn1_w_gate[l].astype(BF16), ffn1_w_up[l].astype(BF16),
               ffn1_w_down[l].astype(BF16), final_norm[l][None, :], final_norm=False)
    x2d = _mixer(x2d, mix_norm[l][None, :], w_in[l].astype(BF16), qg, kg, bd,
                 _bias_table(rel_bias[l]), pw_bd, pool_scale[l][None, :], w_out[l].astype(BF16))
    x2d = _ffn(x2d, ffn2_norm[l][None, :], ffn2_w_gate[l].astype(BF16), ffn2_w_up[l].astype(BF16),
               ffn2_w_down[l].astype(BF16), final_norm[l][None, :], final_norm=True)
    return x2d.reshape(b, s, d)
```

```python
import functools
import math

import numpy as np
import jax
import jax.numpy as jnp
from jax import lax
from jax.experimental import pallas as pl
from jax.experimental.pallas import tpu as pltpu

D_MODEL = 1024
SEQ = 4096
CHUNK = 64
N_LEFT_CHUNKS = 8
D_ATTN = 512
HEAD_DIM = 64
N_HEADS = 8
D_POOL = 512
POOL_WINDOWS = (2, 4, 8, 16)
POOL_GROUP_DIM = 128
REL_CLIP = 128
D_FF = 2816
D_IN = 3 * D_ATTN + D_POOL
EPS = 1e-6
NEG_INF = -1e30

LANES = 128
MXU_DIM = 256
VMEM_LIMIT_BYTES = 56 * 1024 * 1024

TM_FFN = 1024
TM_MIX = 512
TILES_PER_SEQ = SEQ // TM_MIX
Q_BLOCK = 256
SPAN = N_LEFT_CHUNKS * CHUNK
WINDOW = Q_BLOCK + SPAN
KEY_GROUP = 128
BAND_GROUPS = (SPAN + 2 * CHUNK) // KEY_GROUP
UNITS = N_HEADS * TM_MIX // Q_BLOCK
POOL_HALO = 16
LOG2E = math.log2(math.e)

BF16 = jnp.bfloat16
F32 = jnp.float32


def _rms(x, gain):
    return x * lax.rsqrt(jnp.mean(x * x, axis=-1, keepdims=True) + EPS) * gain


def _dot(a, b):
    return jnp.dot(a, b, preferred_element_type=F32)


def _ffn_kernel(x_ref, g_ref, wg_ref, wu_ref, wd_ref, fg_ref, o_ref, *, final_norm):
    x = x_ref[...]
    h = _rms(x, g_ref[...]).astype(BF16)
    gate = _dot(h, wg_ref[...])
    up = _dot(h, wu_ref[...])
    act = (gate * jax.nn.sigmoid(gate) * up).astype(BF16)
    y = x + 0.5 * _dot(act, wd_ref[...])
    if final_norm:
        y = _rms(y, fg_ref[...])
    o_ref[...] = y


def _resident(shape):
    return pl.BlockSpec(shape, lambda i: (0,) * len(shape), pipeline_mode=pl.Buffered(1))


def _ffn(x2d, norm_g, wg, wu, wd, final_g, *, final_norm):
    n = x2d.shape[0]
    return pl.pallas_call(
        functools.partial(_ffn_kernel, final_norm=final_norm),
        grid=(n // TM_FFN,),
        in_specs=[
            pl.BlockSpec((TM_FFN, D_MODEL), lambda i: (i, 0)),
            _resident((1, D_MODEL)),
            _resident((D_MODEL, D_FF)),
            _resident((D_MODEL, D_FF)),
            _resident((D_FF, D_MODEL)),
            _resident((1, D_MODEL)),
        ],
        out_specs=pl.BlockSpec((TM_FFN, D_MODEL), lambda i: (i, 0)),
        out_shape=jax.ShapeDtypeStruct((n, D_MODEL), F32),
        compiler_params=pltpu.CompilerParams(
            dimension_semantics=("arbitrary",), vmem_limit_bytes=VMEM_LIMIT_BYTES),
        name="ffn_final" if final_norm else "ffn",
    )(x2d, norm_g, wg, wu, wd, final_g)


def _head_rms(t, gain, bd):
    sq = t * t
    hi = sq.astype(BF16)
    lo = (sq - hi.astype(F32)).astype(BF16)
    parts = []
    for s in range(D_ATTN // MXU_DIM):
        sl = slice(s * MXU_DIM, (s + 1) * MXU_DIM)
        parts.append(_dot(hi[:, sl], bd) + _dot(lo[:, sl], bd))
    ms = jnp.concatenate(parts, axis=1)
    return t * lax.rsqrt(ms + EPS) * gain


def _mixer_kernel(x_ref, g_ref, wku_ref, wqvt_ref, qg_ref, kg_ref, bd_ref, tblt_ref, pw_ref,
                  ps_ref, wouta_ref, woutp_ref, o_ref,
                  qt_scr, k_scr, vt_scr, halo_scr, d_scr, catt_scr, pt_scr):
    step = pl.program_id(0)
    j = step % TILES_PER_SEQ

    @pl.when(step == 0)
    def _():
        k_scr[...] = jnp.zeros(k_scr.shape, BF16)
        vt_scr[...] = jnp.zeros(vt_scr.shape, BF16)
        halo_scr[...] = jnp.zeros(halo_scr.shape, F32)
        pt_scr[...] = jnp.zeros(pt_scr.shape, BF16)

    parity = step % 2

    x = x_ref[...]
    hf = _rms(x, g_ref[...])
    h = hf.astype(BF16)
    ht = hf.T.astype(BF16)
    ku = _dot(h, wku_ref[...])
    qvt = _dot(wqvt_ref[...], ht)
    k_scr[parity] = _head_rms(ku[:, 0:D_ATTN], kg_ref[...], bd_ref[...]).astype(BF16)
    u = ku[:, D_ATTN:]
    for head in range(N_HEADS):
        rows = slice(head * HEAD_DIM, (head + 1) * HEAD_DIM)
        qh = qvt[rows, :]
        ms = jnp.mean(qh * qh, axis=0, keepdims=True)
        qt_scr[rows, :] = (qh * lax.rsqrt(ms + EPS) * qg_ref[rows, :]).astype(BF16)
    vt_scr[parity] = qvt[D_ATTN:, :].astype(BF16)
    halo = jnp.where(jnp.full((POOL_HALO, D_POOL), j, jnp.int32) == 0, 0.0, halo_scr[1 - parity])
    halo_scr[parity] = u[TM_MIX - POOL_HALO:, :]

    feat = lax.broadcasted_iota(jnp.int32, (2 * HEAD_DIM, Q_BLOCK), 0)
    pen = jnp.where(jnp.full((1, LANES), j, jnp.int32) == 0, NEG_INF, 0.0)

    def pair_scores(blk, pair):
        r0 = blk * Q_BLOCK
        ls = slice(pair * LANES, (pair + 1) * LANES)
        qt2 = qt_scr[ls, r0:r0 + Q_BLOCK]
        zero = jnp.zeros_like(qt2)
        qmt = jnp.concatenate([jnp.where(feat < HEAD_DIM, qt2, zero),
                               jnp.where(feat >= HEAD_DIM, qt2, zero)], axis=1)
        return (_dot(k_scr[1 - parity, r0:, ls], qmt),
                _dot(k_scr[parity, 0:r0 + Q_BLOCK, ls], qmt))

    units = [(blk, pair) for blk in range(TM_MIX // Q_BLOCK) for pair in range(N_HEADS // 2)]
    ahead = pair_scores(*units[0])
    for n, (blk, pair) in enumerate(units):
        st_prev, st_cur = ahead
        if n + 1 < len(units):
            ahead = pair_scores(*units[n + 1])
        r0 = blk * Q_BLOCK
        n_prev = TM_MIX - r0
        qcols = slice(r0, r0 + Q_BLOCK)
        for hh in range(2):
            head = 2 * pair + hh
            p_buf = pt_scr.at[2 * n + hh]
            sums = []
            for half in range(Q_BLOCK // LANES):
                cl = slice(half * LANES, (half + 1) * LANES)
                sl = slice(hh * Q_BLOCK + half * LANES, hh * Q_BLOCK + (half + 1) * LANES)

                def scores(grp):
                    w0 = grp * KEY_GROUP
                    if w0 < n_prev:
                        blk_s = st_prev[w0:w0 + KEY_GROUP, sl] + pen
                    else:
                        blk_s = st_cur[w0 - n_prev:w0 - n_prev + KEY_GROUP, sl]
                    if _TABLE_NEEDED[grp][half]:
                        blk_s = blk_s + tblt_ref[head, w0:w0 + KEY_GROUP, cl]
                    return blk_s

                groups = range(half, half + BAND_GROUPS)
                m = functools.reduce(
                    jnp.maximum, [jnp.max(scores(grp), axis=0, keepdims=True) for grp in groups])
                total = None
                for grp in groups:
                    e = jnp.exp2(scores(grp) - m)
                    part = jnp.sum(e, axis=0, keepdims=True)
                    total = part if total is None else total + part
                    p_buf[grp * KEY_GROUP:(grp + 1) * KEY_GROUP, cl] = e.astype(BF16)
                sums.append(total)
            hrows = slice(head * HEAD_DIM, (head + 1) * HEAD_DIM)
            ot = (_dot(vt_scr[1 - parity, hrows, r0:], p_buf[0:n_prev, :])
                  + _dot(vt_scr[parity, hrows, 0:WINDOW - n_prev], p_buf[n_prev:, :]))
            catt_scr[hrows, qcols] = (ot / jnp.concatenate(sums, axis=1)).astype(BF16)

    pos = lax.broadcasted_iota(jnp.int32, (TM_MIX, 1), 0) + j * TM_MIX
    for g, w in enumerate(POOL_WINDOWS):
        ls = slice(g * POOL_GROUP_DIM, (g + 1) * POOL_GROUP_DIM)
        e = jnp.concatenate([halo[:, ls], u[:, ls]], axis=0)
        acc = e
        sh = 1
        while sh < w:
            acc = acc + pltpu.roll(acc, sh, axis=0)
            sh *= 2
        cnt = jnp.minimum(pos + 1, w).astype(F32)
        d = acc[POOL_HALO:, :] / cnt - u[:, ls]
        d_scr[:, ls] = d.astype(BF16)
    pooled = []
    for pr in range(D_POOL // MXU_DIM):
        sl = slice(pr * MXU_DIM, (pr + 1) * MXU_DIM)
        pooled.append((_dot(d_scr[:, sl], pw_ref[pr]) * ps_ref[:, sl]).astype(BF16))

    attn = lax.dot_general(catt_scr[...], wouta_ref[...], (((0,), (0,)), ((), ())),
                           preferred_element_type=F32)
    o_ref[...] = x + attn + _dot(jnp.concatenate(pooled, axis=1), woutp_ref[...])


def _mixer(x2d, norm_g, w_ku, w_qvt, qg, kg, bd, tblt, pw_bd, ps, w_out_a, w_out_p):
    n = x2d.shape[0]
    return pl.pallas_call(
        _mixer_kernel,
        grid=(n // TM_MIX,),
        in_specs=[
            pl.BlockSpec((TM_MIX, D_MODEL), lambda i: (i, 0)),
            _resident((1, D_MODEL)),
            _resident((D_MODEL, D_ATTN + D_POOL)),
            _resident((2 * D_ATTN, D_MODEL)),
            _resident((D_ATTN, 1)),
            _resident((1, D_ATTN)),
            _resident((MXU_DIM, MXU_DIM)),
            _resident((N_HEADS, WINDOW, Q_BLOCK)),
            _resident((D_POOL // MXU_DIM, MXU_DIM, MXU_DIM)),
            _resident((1, D_POOL)),
            _resident((D_ATTN, D_MODEL)),
            _resident((D_POOL, D_MODEL)),
        ],
        out_specs=pl.BlockSpec((TM_MIX, D_MODEL), lambda i: (i, 0)),
        out_shape=jax.ShapeDtypeStruct((n, D_MODEL), F32),
        scratch_shapes=[
            pltpu.VMEM((D_ATTN, TM_MIX), BF16),
            pltpu.VMEM((2, TM_MIX, D_ATTN), BF16),
            pltpu.VMEM((2, D_ATTN, TM_MIX), BF16),
            pltpu.VMEM((2, POOL_HALO, D_POOL), F32),
            pltpu.VMEM((TM_MIX, D_POOL), BF16),
            pltpu.VMEM((D_ATTN, TM_MIX), BF16),
            pltpu.VMEM((UNITS, WINDOW, Q_BLOCK), BF16),
        ],
        compiler_params=pltpu.CompilerParams(
            dimension_semantics=("arbitrary",), vmem_limit_bytes=VMEM_LIMIT_BYTES),
        name="mixer",
    )(x2d, norm_g, w_ku, w_qvt, qg, kg, bd, tblt, pw_bd, ps, w_out_a, w_out_p)


def _band_masks():
    w = np.arange(WINDOW)[:, None]
    q = np.arange(Q_BLOCK)[None, :]
    band = w - (q // CHUNK) * CHUNK
    in_band = (band >= 0) & (band < SPAN + CHUNK)
    far = in_band & (w - q <= SPAN - REL_CLIP)
    return in_band, far


def _table_needed():
    in_band, far = _band_masks()
    needed = []
    for grp in range(WINDOW // KEY_GROUP):
        rows = slice(grp * KEY_GROUP, (grp + 1) * KEY_GROUP)
        row = []
        for half in range(Q_BLOCK // LANES):
            cols = slice(half * LANES, (half + 1) * LANES)
            inside = half <= grp < half + BAND_GROUPS
            assert inside or not in_band[rows, cols].any()
            row.append(bool(inside and not far[rows, cols].all()))
        needed.append(row)
    return needed


_TABLE_NEEDED = _table_needed()


def _bias_table_t(rel_bias):
    period = 1024
    band_len = SPAN + CHUNK
    n_far = SPAN - REL_CLIP + 1
    n_near = REL_CLIP + CHUNK - 1
    rel = (rel_bias - rel_bias[:, 2 * REL_CLIP:]).astype(F32) * LOG2E
    base = jnp.concatenate([
        jnp.zeros((N_HEADS, n_far), F32),
        rel[:, 2 * REL_CLIP - 1:2 * REL_CLIP - 1 - n_near:-1],
        jnp.zeros((N_HEADS, period - n_far - n_near), F32),
    ], axis=1)
    flat = jnp.tile(base, (1, CHUNK))[:, :CHUNK * (period - 1)]
    chunk_t = flat.reshape(N_HEADS, CHUNK, period - 1)[:, :, :band_len].transpose(0, 2, 1)
    cols = [jnp.pad(chunk_t, ((0, 0), (cq * CHUNK, WINDOW - band_len - cq * CHUNK), (0, 0)),
                    constant_values=NEG_INF) for cq in range(Q_BLOCK // CHUNK)]
    return jnp.concatenate(cols, axis=2)


def _block_diag2(a, b):
    z = jnp.zeros_like(a)
    return jnp.concatenate([jnp.concatenate([a, z], axis=1),
                            jnp.concatenate([z, b], axis=1)], axis=0)


def kernel(x, ffn1_norm, ffn1_w_gate, ffn1_w_up, ffn1_w_down, mix_norm, w_in, q_norm, k_norm,
           rel_bias, pool_w, pool_scale, w_out, ffn2_norm, ffn2_w_gate, ffn2_w_up, ffn2_w_down,
           final_norm):
    b, s, d = x.shape
    assert (s, d) == (SEQ, D_MODEL) and ffn1_norm.shape[0] == 1
    x2d = x.reshape(b * s, d)
    l = 0
    head_id = jnp.arange(MXU_DIM) // HEAD_DIM
    bd = jnp.where(head_id[:, None] == head_id[None, :], 1.0 / HEAD_DIM, 0.0).astype(BF16)
    qg = (jnp.tile(q_norm[l], N_HEADS) * (HEAD_DIM ** -0.5 * LOG2E))[:, None]
    kg = jnp.tile(k_norm[l], N_HEADS)[None, :]
    pw = pool_w[l].astype(BF16)
    pw_bd = jnp.stack([_block_diag2(pw[0], pw[1]), _block_diag2(pw[2], pw[3])])
    w_in_b = w_in[l].astype(BF16)
    wq, wk, wv, wu = (w_in_b[:, i * D_ATTN:(i + 1) * D_ATTN] for i in range(4))
    w_ku = jnp.concatenate([wk, wu], axis=1)
    w_qvt = jnp.concatenate([wq, wv], axis=1).T
    w_out_b = w_out[l].astype(BF16)

    x2d = _ffn(x2d, ffn1_norm[l][None, :], ffn1_w_gate[l].astype(BF16), ffn1_w_up[l].astype(BF16),
               ffn1_w_down[l].astype(BF16), final_norm[l][None, :], final_norm=False)
    x2d = _mixer(x2d, mix_norm[l][None, :], w_ku, w_qvt, qg, kg, bd, _bias_table_t(rel_bias[l]),
                 pw_bd, pool_scale[l][None, :], w_out_b[:D_ATTN], w_out_b[D_ATTN:])
    x2d = _ffn(x2d, ffn2_norm[l][None, :], ffn2_w_gate[l].astype(BF16), ffn2_w_up[l].astype(BF16),
               ffn2_w_down[l].astype(BF16), final_norm[l][None, :], final_norm=True)
    return x2d.reshape(b, s, d)
```

```python
import functools
import math

import numpy as np
import jax
import jax.numpy as jnp
from jax import lax
from jax.experimental import pallas as pl
from jax.experimental.pallas import tpu as pltpu

D_MODEL = 1024
SEQ = 4096
CHUNK = 64
N_LEFT_CHUNKS = 8
D_ATTN = 512
HEAD_DIM = 64
N_HEADS = 8
D_POOL = 512
POOL_WINDOWS = (2, 4, 8, 16)
POOL_GROUP_DIM = 128
REL_CLIP = 128
D_FF = 2816
D_IN = 3 * D_ATTN + D_POOL
EPS = 1e-6
NEG_INF = -1e30

LANES = 128
BF16_SUBLANES = 16
MXU_DIM = 256
VMEM_LIMIT_BYTES = 56 * 1024 * 1024

TM_FFN = 1024
TM_MIX = 512
TILES_PER_SEQ = SEQ // TM_MIX
Q_BLOCK = 256
SPAN = N_LEFT_CHUNKS * CHUNK
WINDOW = Q_BLOCK + SPAN
KEY_GROUP = 128
BAND_GROUPS = (SPAN + 2 * CHUNK) // KEY_GROUP
UNITS = N_HEADS * TM_MIX // Q_BLOCK
POOL_HALO = 16
LOG2E = math.log2(math.e)

BF16 = jnp.bfloat16
F32 = jnp.float32


def _rms(x, gain):
    return x * lax.rsqrt(jnp.mean(x * x, axis=-1, keepdims=True) + EPS) * gain


def _dot(a, b):
    return jnp.dot(a, b, preferred_element_type=F32)


def _ffn_kernel(*refs, final_norm, n_side):
    n_in = 5 + int(final_norm)
    x_ref, g_ref, wg_ref, wu_ref, wd_ref = refs[:5]
    side_in = refs[n_in:n_in + n_side]
    o_ref = refs[n_in + n_side]
    side_out = refs[n_in + n_side + 1:]
    x = x_ref[...]
    h = _rms(x, g_ref[...]).astype(BF16)
    gate = _dot(h, wg_ref[...])
    up = _dot(h, wu_ref[...])
    act = (gate * jax.nn.sigmoid(gate) * up).astype(BF16)
    y = x + 0.5 * _dot(act, wd_ref[...])
    if final_norm:
        y = _rms(y, refs[5][...])
    o_ref[...] = y
    for src, dst in zip(side_in, side_out):
        dst[...] = src[...].astype(BF16)


def _resident(shape):
    return pl.BlockSpec(shape, lambda i: (0,) * len(shape), pipeline_mode=pl.Buffered(1))


def _row_blocks(rows, steps):
    return max(nb for nb in range(1, steps + 1)
               if rows % nb == 0 and (rows // nb) % BF16_SUBLANES == 0)


def _ffn(x2d, norm_g, wg, wu, wd, final_g=None, side=()):
    n = x2d.shape[0]
    steps = n // TM_FFN
    final_norm = final_g is not None
    side_specs = []
    for w in side:
        nb = _row_blocks(w.shape[0], steps)
        side_specs.append(pl.BlockSpec(
            (w.shape[0] // nb, w.shape[1]), lambda i, nb=nb: (jnp.minimum(i, nb - 1), 0)))
    tile = pl.BlockSpec((TM_FFN, D_MODEL), lambda i: (i, 0))
    outs = pl.pallas_call(
        functools.partial(_ffn_kernel, final_norm=final_norm, n_side=len(side)),
        grid=(steps,),
        in_specs=[tile, _resident((1, D_MODEL)), _resident((D_MODEL, D_FF)),
                  _resident((D_MODEL, D_FF)), _resident((D_FF, D_MODEL))]
                 + [_resident((1, D_MODEL))] * final_norm + side_specs,
        out_specs=[tile] + side_specs,
        out_shape=[jax.ShapeDtypeStruct((n, D_MODEL), F32)]
                  + [jax.ShapeDtypeStruct(w.shape, BF16) for w in side],
        compiler_params=pltpu.CompilerParams(
            dimension_semantics=("arbitrary",), vmem_limit_bytes=VMEM_LIMIT_BYTES),
        name="ffn_final" if final_norm else "ffn",
    )(x2d, norm_g, wg, wu, wd, *([final_g] if final_norm else []), *side)
    return outs[0], outs[1:]


def _head_rms(t, gain, bd):
    sq = t * t
    hi = sq.astype(BF16)
    lo = (sq - hi.astype(F32)).astype(BF16)
    parts = []
    for s in range(D_ATTN // MXU_DIM):
        sl = slice(s * MXU_DIM, (s + 1) * MXU_DIM)
        parts.append(_dot(hi[:, sl], bd) + _dot(lo[:, sl], bd))
    ms = jnp.concatenate(parts, axis=1)
    return t * lax.rsqrt(ms + EPS) * gain


def _mixer_kernel(x_ref, g_ref, wku_ref, wqvt_ref, qg_ref, kg_ref, bd_ref, tblt_ref, pw_ref,
                  ps_ref, wouta_ref, woutp_ref, o_ref,
                  qt_scr, k_scr, vt_scr, halo_scr, d_scr, catt_scr, pt_scr):
    step = pl.program_id(0)
    j = step % TILES_PER_SEQ

    @pl.when(step == 0)
    def _():
        k_scr[...] = jnp.zeros(k_scr.shape, BF16)
        vt_scr[...] = jnp.zeros(vt_scr.shape, BF16)
        halo_scr[...] = jnp.zeros(halo_scr.shape, F32)
        pt_scr[...] = jnp.zeros(pt_scr.shape, BF16)

    parity = step % 2

    x = x_ref[...]
    hf = _rms(x, g_ref[...])
    h = hf.astype(BF16)
    ht = hf.T.astype(BF16)
    ku = _dot(h, wku_ref[...])
    qvt = _dot(wqvt_ref[...], ht)
    k_scr[parity] = _head_rms(ku[:, 0:D_ATTN], kg_ref[...], bd_ref[...]).astype(BF16)
    u = ku[:, D_ATTN:]
    for head in range(N_HEADS):
        rows = slice(head * HEAD_DIM, (head + 1) * HEAD_DIM)
        qh = qvt[rows, :]
        ms = jnp.mean(qh * qh, axis=0, keepdims=True)
        qt_scr[rows, :] = (qh * lax.rsqrt(ms + EPS) * qg_ref[rows, :]).astype(BF16)
    vt_scr[parity] = qvt[D_ATTN:, :].astype(BF16)
    halo = jnp.where(jnp.full((POOL_HALO, D_POOL), j, jnp.int32) == 0, 0.0, halo_scr[1 - parity])
    halo_scr[parity] = u[TM_MIX - POOL_HALO:, :]

    feat = lax.broadcasted_iota(jnp.int32, (2 * HEAD_DIM, Q_BLOCK), 0)
    pen = jnp.where(jnp.full((1, LANES), j, jnp.int32) == 0, NEG_INF, 0.0)

    def pair_scores(blk, pair):
        r0 = blk * Q_BLOCK
        ls = slice(pair * LANES, (pair + 1) * LANES)
        qt2 = qt_scr[ls, r0:r0 + Q_BLOCK]
        zero = jnp.zeros_like(qt2)
        qmt = jnp.concatenate([jnp.where(feat < HEAD_DIM, qt2, zero),
                               jnp.where(feat >= HEAD_DIM, qt2, zero)], axis=1)
        return (_dot(k_scr[1 - parity, r0:, ls], qmt),
                _dot(k_scr[parity, 0:r0 + Q_BLOCK, ls], qmt))

    units = [(blk, pair) for blk in range(TM_MIX // Q_BLOCK) for pair in range(N_HEADS // 2)]
    ahead = pair_scores(*units[0])
    for n, (blk, pair) in enumerate(units):
        st_prev, st_cur = ahead
        if n + 1 < len(units):
            ahead = pair_scores(*units[n + 1])
        r0 = blk * Q_BLOCK
        n_prev = TM_MIX - r0
        qcols = slice(r0, r0 + Q_BLOCK)
        for hh in range(2):
            head = 2 * pair + hh
            p_buf = pt_scr.at[2 * n + hh]
            sums = []
            for half in range(Q_BLOCK // LANES):
                cl = slice(half * LANES, (half + 1) * LANES)
                sl = slice(hh * Q_BLOCK + half * LANES, hh * Q_BLOCK + (half + 1) * LANES)

                def scores(grp):
                    w0 = grp * KEY_GROUP
                    if w0 < n_prev:
                        blk_s = st_prev[w0:w0 + KEY_GROUP, sl] + pen
                    else:
                        blk_s = st_cur[w0 - n_prev:w0 - n_prev + KEY_GROUP, sl]
                    if _TABLE_NEEDED[grp][half]:
                        blk_s = blk_s + tblt_ref[head, w0:w0 + KEY_GROUP, cl]
                    return blk_s

                groups = range(half, half + BAND_GROUPS)
                m = functools.reduce(
                    jnp.maximum, [jnp.max(scores(grp), axis=0, keepdims=True) for grp in groups])
                total = None
                for grp in groups:
                    e = jnp.exp2(scores(grp) - m)
                    part = jnp.sum(e, axis=0, keepdims=True)
                    total = part if total is None else total + part
                    p_buf[grp * KEY_GROUP:(grp + 1) * KEY_GROUP, cl] = e.astype(BF16)
                sums.append(total)
            hrows = slice(head * HEAD_DIM, (head + 1) * HEAD_DIM)
            ot = (_dot(vt_scr[1 - parity, hrows, r0:], p_buf[0:n_prev, :])
                  + _dot(vt_scr[parity, hrows, 0:WINDOW - n_prev], p_buf[n_prev:, :]))
            catt_scr[hrows, qcols] = (ot / jnp.concatenate(sums, axis=1)).astype(BF16)

    pos = lax.broadcasted_iota(jnp.int32, (TM_MIX, 1), 0) + j * TM_MIX
    for g, w in enumerate(POOL_WINDOWS):
        ls = slice(g * POOL_GROUP_DIM, (g + 1) * POOL_GROUP_DIM)
        e = jnp.concatenate([halo[:, ls], u[:, ls]], axis=0)
        acc = e
        sh = 1
        while sh < w:
            acc = acc + pltpu.roll(acc, sh, axis=0)
            sh *= 2
        cnt = jnp.minimum(pos + 1, w).astype(F32)
        d = acc[POOL_HALO:, :] / cnt - u[:, ls]
        d_scr[:, ls] = d.astype(BF16)
    pooled = []
    for pr in range(D_POOL // MXU_DIM):
        sl = slice(pr * MXU_DIM, (pr + 1) * MXU_DIM)
        pooled.append((_dot(d_scr[:, sl], pw_ref[pr]) * ps_ref[:, sl]).astype(BF16))

    attn = lax.dot_general(catt_scr[...], wouta_ref[...], (((0,), (0,)), ((), ())),
                           preferred_element_type=F32)
    o_ref[...] = x + attn + _dot(jnp.concatenate(pooled, axis=1), woutp_ref[...])


def _mixer(x2d, norm_g, w_ku, w_qvt, qg, kg, bd, tblt, pw_bd, ps, w_out_a, w_out_p):
    n = x2d.shape[0]
    return pl.pallas_call(
        _mixer_kernel,
        grid=(n // TM_MIX,),
        in_specs=[
            pl.BlockSpec((TM_MIX, D_MODEL), lambda i: (i, 0)),
            _resident((1, D_MODEL)),
            _resident((D_MODEL, D_ATTN + D_POOL)),
            _resident((2 * D_ATTN, D_MODEL)),
            _resident((D_ATTN, 1)),
            _resident((1, D_ATTN)),
            _resident((MXU_DIM, MXU_DIM)),
            _resident((N_HEADS, WINDOW, Q_BLOCK)),
            _resident((D_POOL // MXU_DIM, MXU_DIM, MXU_DIM)),
            _resident((1, D_POOL)),
            _resident((D_ATTN, D_MODEL)),
            _resident((D_POOL, D_MODEL)),
        ],
        out_specs=pl.BlockSpec((TM_MIX, D_MODEL), lambda i: (i, 0)),
        out_shape=jax.ShapeDtypeStruct((n, D_MODEL), F32),
        scratch_shapes=[
            pltpu.VMEM((D_ATTN, TM_MIX), BF16),
            pltpu.VMEM((2, TM_MIX, D_ATTN), BF16),
            pltpu.VMEM((2, D_ATTN, TM_MIX), BF16),
            pltpu.VMEM((2, POOL_HALO, D_POOL), F32),
            pltpu.VMEM((TM_MIX, D_POOL), BF16),
            pltpu.VMEM((D_ATTN, TM_MIX), BF16),
            pltpu.VMEM((UNITS, WINDOW, Q_BLOCK), BF16),
        ],
        compiler_params=pltpu.CompilerParams(
            dimension_semantics=("arbitrary",), vmem_limit_bytes=VMEM_LIMIT_BYTES),
        name="mixer",
    )(x2d, norm_g, w_ku, w_qvt, qg, kg, bd, tblt, pw_bd, ps, w_out_a, w_out_p)


def _band_masks():
    w = np.arange(WINDOW)[:, None]
    q = np.arange(Q_BLOCK)[None, :]
    band = w - (q // CHUNK) * CHUNK
    in_band = (band >= 0) & (band < SPAN + CHUNK)
    far = in_band & (w - q <= SPAN - REL_CLIP)
    return in_band, far


def _table_needed():
    in_band, far = _band_masks()
    needed = []
    for grp in range(WINDOW // KEY_GROUP):
        rows = slice(grp * KEY_GROUP, (grp + 1) * KEY_GROUP)
        row = []
        for half in range(Q_BLOCK // LANES):
            cols = slice(half * LANES, (half + 1) * LANES)
            inside = half <= grp < half + BAND_GROUPS
            assert inside or not in_band[rows, cols].any()
            row.append(bool(inside and not far[rows, cols].all()))
        needed.append(row)
    return needed


_TABLE_NEEDED = _table_needed()


def _bias_table_t(rel_bias):
    period = 1024
    band_len = SPAN + CHUNK
    n_far = SPAN - REL_CLIP + 1
    n_near = REL_CLIP + CHUNK - 1
    rel = (rel_bias - rel_bias[:, 2 * REL_CLIP:]).astype(F32) * LOG2E
    base = jnp.concatenate([
        jnp.zeros((N_HEADS, n_far), F32),
        rel[:, 2 * REL_CLIP - 1:2 * REL_CLIP - 1 - n_near:-1],
        jnp.zeros((N_HEADS, period - n_far - n_near), F32),
    ], axis=1)
    flat = jnp.tile(base, (1, CHUNK))[:, :CHUNK * (period - 1)]
    chunk_t = flat.reshape(N_HEADS, CHUNK, period - 1)[:, :, :band_len].transpose(0, 2, 1)
    cols = [jnp.pad(chunk_t, ((0, 0), (cq * CHUNK, WINDOW - band_len - cq * CHUNK), (0, 0)),
                    constant_values=NEG_INF) for cq in range(Q_BLOCK // CHUNK)]
    return jnp.concatenate(cols, axis=2)


def _block_diag2(a, b):
    z = jnp.zeros_like(a)
    return jnp.concatenate([jnp.concatenate([a, z], axis=1),
                            jnp.concatenate([z, b], axis=1)], axis=0)


def kernel(x, ffn1_norm, ffn1_w_gate, ffn1_w_up, ffn1_w_down, mix_norm, w_in, q_norm, k_norm,
           rel_bias, pool_w, pool_scale, w_out, ffn2_norm, ffn2_w_gate, ffn2_w_up, ffn2_w_down,
           final_norm):
    b, s, d = x.shape
    assert (s, d) == (SEQ, D_MODEL) and ffn1_norm.shape[0] == 1
    x2d = x.reshape(b * s, d)
    l = 0
    head_id = jnp.arange(MXU_DIM) // HEAD_DIM
    bd = jnp.where(head_id[:, None] == head_id[None, :], 1.0 / HEAD_DIM, 0.0).astype(BF16)
    qg = (jnp.tile(q_norm[l], N_HEADS) * (HEAD_DIM ** -0.5 * LOG2E))[:, None]
    kg = jnp.tile(k_norm[l], N_HEADS)[None, :]
    pw = pool_w[l].astype(BF16)
    pw_bd = jnp.stack([_block_diag2(pw[0], pw[1]), _block_diag2(pw[2], pw[3])])
    x2d, (w_in_b, w_out_b, wg2, wu2, wd2) = _ffn(
        x2d, ffn1_norm[l][None, :], ffn1_w_gate[l].astype(BF16), ffn1_w_up[l].astype(BF16),
        ffn1_w_down[l].astype(BF16),
        side=(w_in[l], w_out[l], ffn2_w_gate[l], ffn2_w_up[l], ffn2_w_down[l]))
    wq, wk, wv, wu = (w_in_b[:, i * D_ATTN:(i + 1) * D_ATTN] for i in range(4))
    w_ku = jnp.concatenate([wk, wu], axis=1)
    w_qvt = jnp.concatenate([wq, wv], axis=1).T
    x2d = _mixer(x2d, mix_norm[l][None, :], w_ku, w_qvt, qg, kg, bd, _bias_table_t(rel_bias[l]),
                 pw_bd, pool_scale[l][None, :], w_out_b[:D_ATTN], w_out_b[D_ATTN:])
    x2d, _ = _ffn(x2d, ffn2_norm[l][None, :], wg2, wu2, wd2, final_g=final_norm[l][None, :])
    return x2d.reshape(b, s, d)
```

```python
import functools
import math

import numpy as np
import jax
import jax.numpy as jnp
from jax import lax
from jax.experimental import pallas as pl
from jax.experimental.pallas import tpu as pltpu

D_MODEL = 1024
SEQ = 4096
CHUNK = 64
N_LEFT_CHUNKS = 8
D_ATTN = 512
HEAD_DIM = 64
N_HEADS = 8
D_POOL = 512
POOL_WINDOWS = (2, 4, 8, 16)
POOL_GROUP_DIM = 128
REL_CLIP = 128
D_FF = 2816
D_IN = 3 * D_ATTN + D_POOL
EPS = 1e-6
NEG_INF = -1e30

LANES = 128
BF16_SUBLANES = 16
MXU_DIM = 256
VMEM_LIMIT_BYTES = 56 * 1024 * 1024

TM_FFN = 1024
TM_MIX = 512
TILES_PER_SEQ = SEQ // TM_MIX
Q_BLOCK = 256
SPAN = N_LEFT_CHUNKS * CHUNK
WINDOW = Q_BLOCK + SPAN
KEY_GROUP = 128
BAND_GROUPS = (SPAN + 2 * CHUNK) // KEY_GROUP
UNITS = N_HEADS * TM_MIX // Q_BLOCK
POOL_HALO = 16
LOG2E = math.log2(math.e)

BF16 = jnp.bfloat16
F32 = jnp.float32


def _rms(x, gain):
    return x * lax.rsqrt(jnp.mean(x * x, axis=-1, keepdims=True) + EPS) * gain


def _dot(a, b):
    return jnp.dot(a, b, preferred_element_type=F32)


def _ffn_kernel(*refs, final_norm, n_side):
    n_in = 5 + int(final_norm)
    x_ref, g_ref, wg_ref, wu_ref, wd_ref = refs[:5]
    side_in = refs[n_in:n_in + n_side]
    o_ref = refs[n_in + n_side]
    side_out = refs[n_in + n_side + 1:]
    x = x_ref[...]
    h = _rms(x, g_ref[...]).astype(BF16)
    gate = _dot(h, wg_ref[...])
    up = _dot(h, wu_ref[...])
    act = (gate * jax.nn.sigmoid(gate) * up).astype(BF16)
    y = x + 0.5 * _dot(act, wd_ref[...])
    if final_norm:
        y = _rms(y, refs[5][...])
    o_ref[...] = y
    for src, dst in zip(side_in, side_out):
        dst[...] = src[...].astype(BF16)


def _resident(shape):
    return pl.BlockSpec(shape, lambda i: (0,) * len(shape), pipeline_mode=pl.Buffered(1))


def _row_blocks(rows, steps):
    return max(nb for nb in range(1, steps + 1)
               if rows % nb == 0 and (rows // nb) % BF16_SUBLANES == 0)


def _ffn(x2d, norm_g, wg, wu, wd, final_g=None, side=()):
    n = x2d.shape[0]
    steps = n // TM_FFN
    final_norm = final_g is not None
    side_specs = []
    for w in side:
        nb = _row_blocks(w.shape[0], steps)
        side_specs.append(pl.BlockSpec(
            (w.shape[0] // nb, w.shape[1]), lambda i, nb=nb: (jnp.minimum(i, nb - 1), 0)))
    tile = pl.BlockSpec((TM_FFN, D_MODEL), lambda i: (i, 0))
    outs = pl.pallas_call(
        functools.partial(_ffn_kernel, final_norm=final_norm, n_side=len(side)),
        grid=(steps,),
        in_specs=[tile, _resident((1, D_MODEL)), _resident((D_MODEL, D_FF)),
                  _resident((D_MODEL, D_FF)), _resident((D_FF, D_MODEL))]
                 + [_resident((1, D_MODEL))] * final_norm + side_specs,
        out_specs=[tile] + side_specs,
        out_shape=[jax.ShapeDtypeStruct((n, D_MODEL), F32)]
                  + [jax.ShapeDtypeStruct(w.shape, BF16) for w in side],
        compiler_params=pltpu.CompilerParams(
            dimension_semantics=("arbitrary",), vmem_limit_bytes=VMEM_LIMIT_BYTES),
        name="ffn_final" if final_norm else "ffn",
    )(x2d, norm_g, wg, wu, wd, *([final_g] if final_norm else []), *side)
    return outs[0], outs[1:]


def _head_rms(t, gain, bd):
    sq = t * t
    hi = sq.astype(BF16)
    lo = (sq - hi.astype(F32)).astype(BF16)
    parts = []
    for s in range(D_ATTN // MXU_DIM):
        sl = slice(s * MXU_DIM, (s + 1) * MXU_DIM)
        parts.append(_dot(hi[:, sl], bd) + _dot(lo[:, sl], bd))
    ms = jnp.concatenate(parts, axis=1)
    return t * lax.rsqrt(ms + EPS) * gain


def _mixer_kernel(x_ref, g_ref, wku_ref, wqvt_ref, qg_ref, kg_ref, bd_ref, tblt_ref, pw_ref,
                  ps_ref, wouta_ref, woutp_ref, o_ref,
                  qt_scr, k_scr, vt_scr, halo_scr, d_scr, catt_scr, pt_scr):
    step = pl.program_id(0)
    j = step % TILES_PER_SEQ

    @pl.when(step == 0)
    def _():
        k_scr[...] = jnp.zeros(k_scr.shape, BF16)
        vt_scr[...] = jnp.zeros(vt_scr.shape, BF16)
        halo_scr[...] = jnp.zeros(halo_scr.shape, F32)
        pt_scr[...] = jnp.zeros(pt_scr.shape, BF16)

    parity = step % 2

    x = x_ref[...]
    hf = _rms(x, g_ref[...])
    h = hf.astype(BF16)
    ht = hf.T.astype(BF16)
    ku = _dot(h, wku_ref[...])
    qvt = _dot(wqvt_ref[...], ht)
    k_scr[parity] = _head_rms(ku[:, 0:D_ATTN], kg_ref[...], bd_ref[...]).astype(BF16)
    u = ku[:, D_ATTN:]
    for head in range(N_HEADS):
        rows = slice(head * HEAD_DIM, (head + 1) * HEAD_DIM)
        qh = qvt[rows, :]
        ms = jnp.mean(qh * qh, axis=0, keepdims=True)
        qt_scr[rows, :] = (qh * lax.rsqrt(ms + EPS) * qg_ref[rows, :]).astype(BF16)
    vt_scr[parity] = qvt[D_ATTN:, :].astype(BF16)
    halo = jnp.where(jnp.full((POOL_HALO, D_POOL), j, jnp.int32) == 0, 0.0, halo_scr[1 - parity])
    halo_scr[parity] = u[TM_MIX - POOL_HALO:, :]

    feat = lax.broadcasted_iota(jnp.int32, (2 * HEAD_DIM, Q_BLOCK), 0)
    pen = jnp.where(jnp.full((1, LANES), j, jnp.int32) == 0, NEG_INF, 0.0)

    def pair_scores(blk, pair):
        r0 = blk * Q_BLOCK
        n_prev = TM_MIX - r0
        ls = slice(pair * LANES, (pair + 1) * LANES)
        qt2 = qt_scr[ls, r0:r0 + Q_BLOCK]
        zero = jnp.zeros_like(qt2)
        even = jnp.where(feat < HEAD_DIM, qt2, zero)
        odd = jnp.where(feat >= HEAD_DIM, qt2, zero)
        out = []
        for half in range(Q_BLOCK // LANES):
            cl = slice(half * LANES, (half + 1) * LANES)
            qmt = jnp.concatenate([even[:, cl], odd[:, cl]], axis=1)
            lo = half * KEY_GROUP
            hi = lo + BAND_GROUPS * KEY_GROUP
            out.append((_dot(k_scr[1 - parity, r0 + lo:, ls], qmt),
                        _dot(k_scr[parity, 0:hi - n_prev, ls], qmt)))
        return out

    units = [(blk, pair) for blk in range(TM_MIX // Q_BLOCK) for pair in range(N_HEADS // 2)]
    ahead = pair_scores(*units[0])
    for n, (blk, pair) in enumerate(units):
        st = ahead
        if n + 1 < len(units):
            ahead = pair_scores(*units[n + 1])
        r0 = blk * Q_BLOCK
        n_prev = TM_MIX - r0
        qcols = slice(r0, r0 + Q_BLOCK)
        for hh in range(2):
            head = 2 * pair + hh
            p_buf = pt_scr.at[2 * n + hh]
            sums = []
            for half in range(Q_BLOCK // LANES):
                cl = slice(half * LANES, (half + 1) * LANES)
                sl = slice(hh * LANES, (hh + 1) * LANES)
                st_prev, st_cur = st[half]
                lo = half * KEY_GROUP

                def scores(grp):
                    w0 = grp * KEY_GROUP
                    if w0 < n_prev:
                        blk_s = st_prev[w0 - lo:w0 - lo + KEY_GROUP, sl] + pen
                    else:
                        blk_s = st_cur[w0 - n_prev:w0 - n_prev + KEY_GROUP, sl]
                    if _TABLE_NEEDED[grp][half]:
                        blk_s = blk_s + tblt_ref[head, w0:w0 + KEY_GROUP, cl]
                    return blk_s

                groups = range(half, half + BAND_GROUPS)
                m = functools.reduce(
                    jnp.maximum, [jnp.max(scores(grp), axis=0, keepdims=True) for grp in groups])
                total = None
                for grp in groups:
                    e = jnp.exp2(scores(grp) - m)
                    part = jnp.sum(e, axis=0, keepdims=True)
                    total = part if total is None else total + part
                    p_buf[grp * KEY_GROUP:(grp + 1) * KEY_GROUP, cl] = e.astype(BF16)
                sums.append(total)
            hrows = slice(head * HEAD_DIM, (head + 1) * HEAD_DIM)
            ot = (_dot(vt_scr[1 - parity, hrows, r0:], p_buf[0:n_prev, :])
                  + _dot(vt_scr[parity, hrows, 0:WINDOW - n_prev], p_buf[n_prev:, :]))
            catt_scr[hrows, qcols] = (ot / jnp.concatenate(sums, axis=1)).astype(BF16)

    pos = lax.broadcasted_iota(jnp.int32, (TM_MIX, 1), 0) + j * TM_MIX
    for g, w in enumerate(POOL_WINDOWS):
        ls = slice(g * POOL_GROUP_DIM, (g + 1) * POOL_GROUP_DIM)
        e = jnp.concatenate([halo[:, ls], u[:, ls]], axis=0)
        acc = e
        sh = 1
        while sh < w:
            acc = acc + pltpu.roll(acc, sh, axis=0)
            sh *= 2
        cnt = jnp.minimum(pos + 1, w).astype(F32)
        d = acc[POOL_HALO:, :] / cnt - u[:, ls]
        d_scr[:, ls] = d.astype(BF16)
    pooled = []
    for pr in range(D_POOL // MXU_DIM):
        sl = slice(pr * MXU_DIM, (pr + 1) * MXU_DIM)
        pooled.append((_dot(d_scr[:, sl], pw_ref[pr]) * ps_ref[:, sl]).astype(BF16))

    attn = lax.dot_general(catt_scr[...], wouta_ref[...], (((0,), (0,)), ((), ())),
                           preferred_element_type=F32)
    o_ref[...] = x + attn + _dot(jnp.concatenate(pooled, axis=1), woutp_ref[...])


def _mixer(x2d, norm_g, w_ku, w_qvt, qg, kg, bd, tblt, pw_bd, ps, w_out_a, w_out_p):
    n = x2d.shape[0]
    return pl.pallas_call(
        _mixer_kernel,
        grid=(n // TM_MIX,),
        in_specs=[
            pl.BlockSpec((TM_MIX, D_MODEL), lambda i: (i, 0)),
            _resident((1, D_MODEL)),
            _resident((D_MODEL, D_ATTN + D_POOL)),
            _resident((2 * D_ATTN, D_MODEL)),
            _resident((D_ATTN, 1)),
            _resident((1, D_ATTN)),
            _resident((MXU_DIM, MXU_DIM)),
            _resident((N_HEADS, WINDOW, Q_BLOCK)),
            _resident((D_POOL // MXU_DIM, MXU_DIM, MXU_DIM)),
            _resident((1, D_POOL)),
            _resident((D_ATTN, D_MODEL)),
            _resident((D_POOL, D_MODEL)),
        ],
        out_specs=pl.BlockSpec((TM_MIX, D_MODEL), lambda i: (i, 0)),
        out_shape=jax.ShapeDtypeStruct((n, D_MODEL), F32),
        scratch_shapes=[
            pltpu.VMEM((D_ATTN, TM_MIX), BF16),
            pltpu.VMEM((2, TM_MIX, D_ATTN), BF16),
            pltpu.VMEM((2, D_ATTN, TM_MIX), BF16),
            pltpu.VMEM((2, POOL_HALO, D_POOL), F32),
            pltpu.VMEM((TM_MIX, D_POOL), BF16),
            pltpu.VMEM((D_ATTN, TM_MIX), BF16),
            pltpu.VMEM((UNITS, WINDOW, Q_BLOCK), BF16),
        ],
        compiler_params=pltpu.CompilerParams(
            dimension_semantics=("arbitrary",), vmem_limit_bytes=VMEM_LIMIT_BYTES),
        name="mixer",
    )(x2d, norm_g, w_ku, w_qvt, qg, kg, bd, tblt, pw_bd, ps, w_out_a, w_out_p)


def _band_masks():
    w = np.arange(WINDOW)[:, None]
    q = np.arange(Q_BLOCK)[None, :]
    band = w - (q // CHUNK) * CHUNK
    in_band = (band >= 0) & (band < SPAN + CHUNK)
    far = in_band & (w - q <= SPAN - REL_CLIP)
    return in_band, far


def _table_needed():
    in_band, far = _band_masks()
    needed = []
    for grp in range(WINDOW // KEY_GROUP):
        rows = slice(grp * KEY_GROUP, (grp + 1) * KEY_GROUP)
        row = []
        for half in range(Q_BLOCK // LANES):
            cols = slice(half * LANES, (half + 1) * LANES)
            inside = half <= grp < half + BAND_GROUPS
            assert inside or not in_band[rows, cols].any()
            row.append(bool(inside and not far[rows, cols].all()))
        needed.append(row)
    return needed


_TABLE_NEEDED = _table_needed()


def _bias_table_t(rel_bias):
    period = 1024
    band_len = SPAN + CHUNK
    n_far = SPAN - REL_CLIP + 1
    n_near = REL_CLIP + CHUNK - 1
    rel = (rel_bias - rel_bias[:, 2 * REL_CLIP:]).astype(F32) * LOG2E
    base = jnp.concatenate([
        jnp.zeros((N_HEADS, n_far), F32),
        rel[:, 2 * REL_CLIP - 1:2 * REL_CLIP - 1 - n_near:-1],
        jnp.zeros((N_HEADS, period - n_far - n_near), F32),
    ], axis=1)
    flat = jnp.tile(base, (1, CHUNK))[:, :CHUNK * (period - 1)]
    chunk_t = flat.reshape(N_HEADS, CHUNK, period - 1)[:, :, :band_len].transpose(0, 2, 1)
    cols = [jnp.pad(chunk_t, ((0, 0), (cq * CHUNK, WINDOW - band_len - cq * CHUNK), (0, 0)),
                    constant_values=NEG_INF) for cq in range(Q_BLOCK // CHUNK)]
    return jnp.concatenate(cols, axis=2)


def _block_diag2(a, b):
    z = jnp.zeros_like(a)
    return jnp.concatenate([jnp.concatenate([a, z], axis=1),
                            jnp.concatenate([z, b], axis=1)], axis=0)


def kernel(x, ffn1_norm, ffn1_w_gate, ffn1_w_up, ffn1_w_down, mix_norm, w_in, q_norm, k_norm,
           rel_bias, pool_w, pool_scale, w_out, ffn2_norm, ffn2_w_gate, ffn2_w_up, ffn2_w_down,
           final_norm):
    b, s, d = x.shape
    assert (s, d) == (SEQ, D_MODEL) and ffn1_norm.shape[0] == 1
    x2d = x.reshape(b * s, d)
    l = 0
    head_id = jnp.arange(MXU_DIM) // HEAD_DIM
    bd = jnp.where(head_id[:, None] == head_id[None, :], 1.0 / HEAD_DIM, 0.0).astype(BF16)
    qg = (jnp.tile(q_norm[l], N_HEADS) * (HEAD_DIM ** -0.5 * LOG2E))[:, None]
    kg = jnp.tile(k_norm[l], N_HEADS)[None, :]
    pw = pool_w[l].astype(BF16)
    pw_bd = jnp.stack([_block_diag2(pw[0], pw[1]), _block_diag2(pw[2], pw[3])])
    x2d, (w_in_b, w_out_b, wg2, wu2, wd2) = _ffn(
        x2d, ffn1_norm[l][None, :], ffn1_w_gate[l].astype(BF16), ffn1_w_up[l].astype(BF16),
        ffn1_w_down[l].astype(BF16),
        side=(w_in[l], w_out[l], ffn2_w_gate[l], ffn2_w_up[l], ffn2_w_down[l]))
    wq, wk, wv, wu = (w_in_b[:, i * D_ATTN:(i + 1) * D_ATTN] for i in range(4))
    w_ku = jnp.concatenate([wk, wu], axis=1)
    w_qvt = jnp.concatenate([wq, wv], axis=1).T
    x2d = _mixer(x2d, mix_norm[l][None, :], w_ku, w_qvt, qg, kg, bd, _bias_table_t(rel_bias[l]),
                 pw_bd, pool_scale[l][None, :], w_out_b[:D_ATTN], w_out_b[D_ATTN:])
    x2d, _ = _ffn(x2d, ffn2_norm[l][None, :], wg2, wu2, wd2, final_g=final_norm[l][None, :])
    return x2d.reshape(b, s, d)
```

```python
import functools
import math

import numpy as np
import jax
import jax.numpy as jnp
from jax import lax
from jax.experimental import pallas as pl
from jax.experimental.pallas import tpu as pltpu

D_MODEL = 1024
SEQ = 4096
CHUNK = 64
N_LEFT_CHUNKS = 8
D_ATTN = 512
HEAD_DIM = 64
N_HEADS = 8
D_POOL = 512
POOL_WINDOWS = (2, 4, 8, 16)
POOL_GROUP_DIM = 128
REL_CLIP = 128
D_FF = 2816
D_IN = 3 * D_ATTN + D_POOL
EPS = 1e-6
NEG_INF = -1e30

LANES = 128
BF16_SUBLANES = 16
MXU_DIM = 256
VMEM_LIMIT_BYTES = 56 * 1024 * 1024

TM_FFN = 1024
TM_MIX = 512
TILES_PER_SEQ = SEQ // TM_MIX
Q_BLOCK = 256
SPAN = N_LEFT_CHUNKS * CHUNK
WINDOW = Q_BLOCK + SPAN
KEY_GROUP = 128
BAND_GROUPS = (SPAN + 2 * CHUNK) // KEY_GROUP
UNITS = N_HEADS * TM_MIX // Q_BLOCK
POOL_HALO = 16
LOG2E = math.log2(math.e)

BF16 = jnp.bfloat16
F32 = jnp.float32


def _rms(x, gain):
    return x * lax.rsqrt(jnp.mean(x * x, axis=-1, keepdims=True) + EPS) * gain


def _dot(a, b):
    return jnp.dot(a, b, preferred_element_type=F32)


def _ffn_kernel(*refs, final_norm, n_side):
    n_in = 5 + int(final_norm)
    x_ref, g_ref, wg_ref, wu_ref, wd_ref = refs[:5]
    side_in = refs[n_in:n_in + n_side]
    o_ref = refs[n_in + n_side]
    side_out = refs[n_in + n_side + 1:]
    x = x_ref[...]
    h = _rms(x, g_ref[...]).astype(BF16)
    gate = _dot(h, wg_ref[...])
    up = _dot(h, wu_ref[...])
    act = (gate * jax.nn.sigmoid(gate) * up).astype(BF16)
    y = x + 0.5 * _dot(act, wd_ref[...])
    if final_norm:
        y = _rms(y, refs[5][...])
    o_ref[...] = y
    for src, dst in zip(side_in, side_out):
        dst[...] = src[...].astype(BF16)


def _resident(shape):
    return pl.BlockSpec(shape, lambda i: (0,) * len(shape), pipeline_mode=pl.Buffered(1))


def _row_blocks(rows, steps):
    return max(nb for nb in range(1, steps + 1)
               if rows % nb == 0 and (rows // nb) % BF16_SUBLANES == 0)


def _ffn(x2d, norm_g, wg, wu, wd, final_g=None, side=()):
    n = x2d.shape[0]
    steps = n // TM_FFN
    final_norm = final_g is not None
    side_specs = []
    for w in side:
        nb = _row_blocks(w.shape[0], steps)
        side_specs.append(pl.BlockSpec(
            (w.shape[0] // nb, w.shape[1]), lambda i, nb=nb: (jnp.minimum(i, nb - 1), 0)))
    tile = pl.BlockSpec((TM_FFN, D_MODEL), lambda i: (i, 0))
    outs = pl.pallas_call(
        functools.partial(_ffn_kernel, final_norm=final_norm, n_side=len(side)),
        grid=(steps,),
        in_specs=[tile, _resident((1, D_MODEL)), _resident((D_MODEL, D_FF)),
                  _resident((D_MODEL, D_FF)), _resident((D_FF, D_MODEL))]
                 + [_resident((1, D_MODEL))] * final_norm + side_specs,
        out_specs=[tile] + side_specs,
        out_shape=[jax.ShapeDtypeStruct((n, D_MODEL), F32)]
                  + [jax.ShapeDtypeStruct(w.shape, BF16) for w in side],
        compiler_params=pltpu.CompilerParams(
            dimension_semantics=("arbitrary",), vmem_limit_bytes=VMEM_LIMIT_BYTES),
        name="ffn_final" if final_norm else "ffn",
    )(x2d, norm_g, wg, wu, wd, *([final_g] if final_norm else []), *side)
    return outs[0], outs[1:]


def _head_rms(t, gain, bd):
    sq = t * t
    hi = sq.astype(BF16)
    lo = (sq - hi.astype(F32)).astype(BF16)
    parts = []
    for s in range(D_ATTN // MXU_DIM):
        sl = slice(s * MXU_DIM, (s + 1) * MXU_DIM)
        parts.append(_dot(hi[:, sl], bd) + _dot(lo[:, sl], bd))
    ms = jnp.concatenate(parts, axis=1)
    return t * lax.rsqrt(ms + EPS) * gain


def _mixer_kernel(x_ref, g_ref, wku_ref, wqvt_ref, qg_ref, kg_ref, bd_ref, tblt_ref, pw_ref,
                  ps_ref, wouta_ref, woutp_ref, o_ref,
                  qt_scr, k_scr, vt_scr, halo_scr, d_scr, catt_scr, pt_scr):
    step = pl.program_id(0)
    j = step % TILES_PER_SEQ

    @pl.when(step == 0)
    def _():
        k_scr[...] = jnp.zeros(k_scr.shape, BF16)
        vt_scr[...] = jnp.zeros(vt_scr.shape, BF16)
        halo_scr[...] = jnp.zeros(halo_scr.shape, F32)
        pt_scr[...] = jnp.zeros(pt_scr.shape, BF16)

    parity = step % 2

    x = x_ref[...]
    hf = _rms(x, g_ref[...])
    h = hf.astype(BF16)
    ht = hf.T.astype(BF16)
    ku = _dot(h, wku_ref[...])
    qvt = _dot(wqvt_ref[...], ht)
    k_scr[parity] = _head_rms(ku[:, 0:D_ATTN], kg_ref[...], bd_ref[...]).astype(BF16)
    u = ku[:, D_ATTN:]
    for head in range(N_HEADS):
        rows = slice(head * HEAD_DIM, (head + 1) * HEAD_DIM)
        qh = qvt[rows, :]
        ms = jnp.mean(qh * qh, axis=0, keepdims=True)
        qt_scr[rows, :] = (qh * lax.rsqrt(ms + EPS) * qg_ref[rows, :]).astype(BF16)
    vt_scr[parity] = qvt[D_ATTN:, :].astype(BF16)
    halo = jnp.where(jnp.full((POOL_HALO, D_POOL), j, jnp.int32) == 0, 0.0, halo_scr[1 - parity])
    halo_scr[parity] = u[TM_MIX - POOL_HALO:, :]

    feat = lax.broadcasted_iota(jnp.int32, (2 * HEAD_DIM, Q_BLOCK), 0)
    pen = jnp.where(jnp.full((1, LANES), j, jnp.int32) == 0, NEG_INF, 0.0)

    def pair_scores(blk, pair):
        r0 = blk * Q_BLOCK
        n_prev = TM_MIX - r0
        ls = slice(pair * LANES, (pair + 1) * LANES)
        qt2 = qt_scr[ls, r0:r0 + Q_BLOCK]
        zero = jnp.zeros_like(qt2)
        even = jnp.where(feat < HEAD_DIM, qt2, zero)
        odd = jnp.where(feat >= HEAD_DIM, qt2, zero)
        out = []
        for half in range(Q_BLOCK // LANES):
            cl = slice(half * LANES, (half + 1) * LANES)
            qmt = jnp.concatenate([even[:, cl], odd[:, cl]], axis=1)
            lo = half * KEY_GROUP
            hi = lo + BAND_GROUPS * KEY_GROUP
            out.append((_dot(k_scr[1 - parity, r0 + lo:, ls], qmt),
                        _dot(k_scr[parity, 0:hi - n_prev, ls], qmt)))
        return out

    units = [(blk, pair) for blk in range(TM_MIX // Q_BLOCK) for pair in range(N_HEADS // 2)]
    ahead = pair_scores(*units[0])
    for n, (blk, pair) in enumerate(units):
        st = ahead
        if n + 1 < len(units):
            ahead = pair_scores(*units[n + 1])
        r0 = blk * Q_BLOCK
        n_prev = TM_MIX - r0
        qcols = slice(r0, r0 + Q_BLOCK)
        for hh in range(2):
            head = 2 * pair + hh
            p_buf = pt_scr.at[2 * n + hh]
            sums = []
            for half in range(Q_BLOCK // LANES):
                cl = slice(half * LANES, (half + 1) * LANES)
                sl = slice(hh * LANES, (hh + 1) * LANES)
                st_prev, st_cur = st[half]
                lo = half * KEY_GROUP

                def scores(grp):
                    w0 = grp * KEY_GROUP
                    if w0 < n_prev:
                        blk_s = st_prev[w0 - lo:w0 - lo + KEY_GROUP, sl]
                    else:
                        blk_s = st_cur[w0 - n_prev:w0 - n_prev + KEY_GROUP, sl]
                    if _TABLE_NEEDED[grp][half]:
                        blk_s = blk_s + tblt_ref[head, w0:w0 + KEY_GROUP, cl]
                    return blk_s

                def group_pen(grp):
                    return pen if grp * KEY_GROUP < n_prev else 0.0

                groups = range(half, half + BAND_GROUPS)
                m = functools.reduce(jnp.maximum, [
                    jnp.max(scores(grp), axis=0, keepdims=True) + group_pen(grp) for grp in groups])
                total = None
                for grp in groups:
                    e = jnp.exp2(scores(grp) - (m - group_pen(grp)))
                    part = jnp.sum(e, axis=0, keepdims=True)
                    total = part if total is None else total + part
                    p_buf[grp * KEY_GROUP:(grp + 1) * KEY_GROUP, cl] = e.astype(BF16)
                sums.append(total)
            hrows = slice(head * HEAD_DIM, (head + 1) * HEAD_DIM)
            ot = (_dot(vt_scr[1 - parity, hrows, r0:], p_buf[0:n_prev, :])
                  + _dot(vt_scr[parity, hrows, 0:WINDOW - n_prev], p_buf[n_prev:, :]))
            catt_scr[hrows, qcols] = (ot / jnp.concatenate(sums, axis=1)).astype(BF16)

    pos = lax.broadcasted_iota(jnp.int32, (TM_MIX, 1), 0) + j * TM_MIX
    for g, w in enumerate(POOL_WINDOWS):
        ls = slice(g * POOL_GROUP_DIM, (g + 1) * POOL_GROUP_DIM)
        e = jnp.concatenate([halo[:, ls], u[:, ls]], axis=0)
        acc = e
        sh = 1
        while sh < w:
            acc = acc + pltpu.roll(acc, sh, axis=0)
            sh *= 2
        cnt = jnp.minimum(pos + 1, w).astype(F32)
        d = acc[POOL_HALO:, :] / cnt - u[:, ls]
        d_scr[:, ls] = d.astype(BF16)
    pooled = []
    for pr in range(D_POOL // MXU_DIM):
        sl = slice(pr * MXU_DIM, (pr + 1) * MXU_DIM)
        pooled.append((_dot(d_scr[:, sl], pw_ref[pr]) * ps_ref[:, sl]).astype(BF16))

    attn = lax.dot_general(catt_scr[...], wouta_ref[...], (((0,), (0,)), ((), ())),
                           preferred_element_type=F32)
    o_ref[...] = x + attn + _dot(jnp.concatenate(pooled, axis=1), woutp_ref[...])


def _mixer(x2d, norm_g, w_ku, w_qvt, qg, kg, bd, tblt, pw_bd, ps, w_out_a, w_out_p):
    n = x2d.shape[0]
    return pl.pallas_call(
        _mixer_kernel,
        grid=(n // TM_MIX,),
        in_specs=[
            pl.BlockSpec((TM_MIX, D_MODEL), lambda i: (i, 0)),
            _resident((1, D_MODEL)),
            _resident((D_MODEL, D_ATTN + D_POOL)),
            _resident((2 * D_ATTN, D_MODEL)),
            _resident((D_ATTN, 1)),
            _resident((1, D_ATTN)),
            _resident((MXU_DIM, MXU_DIM)),
            _resident((N_HEADS, WINDOW, Q_BLOCK)),
            _resident((D_POOL // MXU_DIM, MXU_DIM, MXU_DIM)),
            _resident((1, D_POOL)),
            _resident((D_ATTN, D_MODEL)),
            _resident((D_POOL, D_MODEL)),
        ],
        out_specs=pl.BlockSpec((TM_MIX, D_MODEL), lambda i: (i, 0)),
        out_shape=jax.ShapeDtypeStruct((n, D_MODEL), F32),
        scratch_shapes=[
            pltpu.VMEM((D_ATTN, TM_MIX), BF16),
            pltpu.VMEM((2, TM_MIX, D_ATTN), BF16),
            pltpu.VMEM((2, D_ATTN, TM_MIX), BF16),
            pltpu.VMEM((2, POOL_HALO, D_POOL), F32),
            pltpu.VMEM((TM_MIX, D_POOL), BF16),
            pltpu.VMEM((D_ATTN, TM_MIX), BF16),
            pltpu.VMEM((UNITS, WINDOW, Q_BLOCK), BF16),
        ],
        compiler_params=pltpu.CompilerParams(
            dimension_semantics=("arbitrary",), vmem_limit_bytes=VMEM_LIMIT_BYTES),
        name="mixer",
    )(x2d, norm_g, w_ku, w_qvt, qg, kg, bd, tblt, pw_bd, ps, w_out_a, w_out_p)


def _band_masks():
    w = np.arange(WINDOW)[:, None]
    q = np.arange(Q_BLOCK)[None, :]
    band = w - (q // CHUNK) * CHUNK
    in_band = (band >= 0) & (band < SPAN + CHUNK)
    far = in_band & (w - q <= SPAN - REL_CLIP)
    return in_band, far


def _table_needed():
    in_band, far = _band_masks()
    needed = []
    for grp in range(WINDOW // KEY_GROUP):
        rows = slice(grp * KEY_GROUP, (grp + 1) * KEY_GROUP)
        row = []
        for half in range(Q_BLOCK // LANES):
            cols = slice(half * LANES, (half + 1) * LANES)
            inside = half <= grp < half + BAND_GROUPS
            assert inside or not in_band[rows, cols].any()
            row.append(bool(inside and not far[rows, cols].all()))
        needed.append(row)
    return needed


_TABLE_NEEDED = _table_needed()


def _bias_table_t(rel_bias):
    period = 1024
    band_len = SPAN + CHUNK
    n_far = SPAN - REL_CLIP + 1
    n_near = REL_CLIP + CHUNK - 1
    rel = (rel_bias - rel_bias[:, 2 * REL_CLIP:]).astype(F32) * LOG2E
    base = jnp.concatenate([
        jnp.zeros((N_HEADS, n_far), F32),
        rel[:, 2 * REL_CLIP - 1:2 * REL_CLIP - 1 - n_near:-1],
        jnp.zeros((N_HEADS, period - n_far - n_near), F32),
    ], axis=1)
    flat = jnp.tile(base, (1, CHUNK))[:, :CHUNK * (period - 1)]
    chunk_t = flat.reshape(N_HEADS, CHUNK, period - 1)[:, :, :band_len].transpose(0, 2, 1)
    cols = [jnp.pad(chunk_t, ((0, 0), (cq * CHUNK, WINDOW - band_len - cq * CHUNK), (0, 0)),
                    constant_values=NEG_INF) for cq in range(Q_BLOCK // CHUNK)]
    return jnp.concatenate(cols, axis=2)


def _block_diag2(a, b):
    z = jnp.zeros_like(a)
    return jnp.concatenate([jnp.concatenate([a, z], axis=1),
                            jnp.concatenate([z, b], axis=1)], axis=0)


def kernel(x, ffn1_norm, ffn1_w_gate, ffn1_w_up, ffn1_w_down, mix_norm, w_in, q_norm, k_norm,
           rel_bias, pool_w, pool_scale, w_out, ffn2_norm, ffn2_w_gate, ffn2_w_up, ffn2_w_down,
           final_norm):
    b, s, d = x.shape
    assert (s, d) == (SEQ, D_MODEL) and ffn1_norm.shape[0] == 1
    x2d = x.reshape(b * s, d)
    l = 0
    head_id = jnp.arange(MXU_DIM) // HEAD_DIM
    bd = jnp.where(head_id[:, None] == head_id[None, :], 1.0 / HEAD_DIM, 0.0).astype(BF16)
    qg = (jnp.tile(q_norm[l], N_HEADS) * (HEAD_DIM ** -0.5 * LOG2E))[:, None]
    kg = jnp.tile(k_norm[l], N_HEADS)[None, :]
    pw = pool_w[l].astype(BF16)
    pw_bd = jnp.stack([_block_diag2(pw[0], pw[1]), _block_diag2(pw[2], pw[3])])
    x2d, (w_in_b, w_out_b, wg2, wu2, wd2) = _ffn(
        x2d, ffn1_norm[l][None, :], ffn1_w_gate[l].astype(BF16), ffn1_w_up[l].astype(BF16),
        ffn1_w_down[l].astype(BF16),
        side=(w_in[l], w_out[l], ffn2_w_gate[l], ffn2_w_up[l], ffn2_w_down[l]))
    wq, wk, wv, wu = (w_in_b[:, i * D_ATTN:(i + 1) * D_ATTN] for i in range(4))
    w_ku = jnp.concatenate([wk, wu], axis=1)
    w_qvt = jnp.concatenate([wq, wv], axis=1).T
    x2d = _mixer(x2d, mix_norm[l][None, :], w_ku, w_qvt, qg, kg, bd, _bias_table_t(rel_bias[l]),
                 pw_bd, pool_scale[l][None, :], w_out_b[:D_ATTN], w_out_b[D_ATTN:])
    x2d, _ = _ffn(x2d, ffn2_norm[l][None, :], wg2, wu2, wd2, final_g=final_norm[l][None, :])
    return x2d.reshape(b, s, d)
```

```python
import functools
import math

import numpy as np
import jax
import jax.numpy as jnp
from jax import lax
from jax.experimental import pallas as pl
from jax.experimental.pallas import tpu as pltpu

D_MODEL = 1024
SEQ = 4096
CHUNK = 64
N_LEFT_CHUNKS = 8
D_ATTN = 512
HEAD_DIM = 64
N_HEADS = 8
D_POOL = 512
POOL_WINDOWS = (2, 4, 8, 16)
POOL_GROUP_DIM = 128
REL_CLIP = 128
D_FF = 2816
D_IN = 3 * D_ATTN + D_POOL
EPS = 1e-6
NEG_INF = -1e30

LANES = 128
BF16_SUBLANES = 16
MXU_DIM = 256
VMEM_LIMIT_BYTES = 56 * 1024 * 1024

TM_FFN = 1024
TM_MIX = 512
TILES_PER_SEQ = SEQ // TM_MIX
Q_BLOCK = 256
SPAN = N_LEFT_CHUNKS * CHUNK
WINDOW = Q_BLOCK + SPAN
KEY_GROUP = 128
BAND_GROUPS = (SPAN + 2 * CHUNK) // KEY_GROUP
UNITS = N_HEADS * TM_MIX // Q_BLOCK
POOL_HALO = 16
LOG2E = math.log2(math.e)

BF16 = jnp.bfloat16
F32 = jnp.float32


def _rms(x, gain):
    return x * lax.rsqrt(jnp.mean(x * x, axis=-1, keepdims=True) + EPS) * gain


def _dot(a, b):
    return jnp.dot(a, b, preferred_element_type=F32)


def _ffn_kernel(*refs, final_norm, n_side, n_side_t):
    n_in = 5 + int(final_norm)
    x_ref, g_ref, wg_ref, wu_ref, wd_ref = refs[:5]
    side_in = refs[n_in:n_in + n_side]
    o_ref = refs[n_in + n_side]
    side_out = refs[n_in + n_side + 1:]
    x = x_ref[...]
    h = _rms(x, g_ref[...]).astype(BF16)
    gate = _dot(h, wg_ref[...])
    up = _dot(h, wu_ref[...])
    act = (gate * jax.nn.sigmoid(gate) * up).astype(BF16)
    y = x + 0.5 * _dot(act, wd_ref[...])
    if final_norm:
        y = _rms(y, refs[5][...])
    o_ref[...] = y
    for k, (src, dst) in enumerate(zip(side_in, side_out)):
        w = src[...]
        dst[...] = (w.T if k >= n_side - n_side_t else w).astype(BF16)


def _resident(shape):
    return pl.BlockSpec(shape, lambda i: (0,) * len(shape), pipeline_mode=pl.Buffered(1))


def _row_blocks(rows, steps):
    return max(nb for nb in range(1, steps + 1)
               if rows % nb == 0 and (rows // nb) % BF16_SUBLANES == 0)


def _ffn(x2d, norm_g, wg, wu, wd, final_g=None, side=(), side_t=()):
    n = x2d.shape[0]
    steps = n // TM_FFN
    final_norm = final_g is not None
    in_side, out_side, out_shapes = [], [], []
    for w in side:
        nb = _row_blocks(w.shape[0], steps)
        spec = pl.BlockSpec(
            (w.shape[0] // nb, w.shape[1]), lambda i, nb=nb: (jnp.minimum(i, nb - 1), 0))
        in_side.append(spec)
        out_side.append(spec)
        out_shapes.append(jax.ShapeDtypeStruct(w.shape, BF16))
    for w in side_t:
        nb = w.shape[0] // LANES
        in_side.append(pl.BlockSpec(
            (LANES, w.shape[1]), lambda i, nb=nb: (jnp.minimum(i, nb - 1), 0)))
        out_side.append(pl.BlockSpec(
            (w.shape[1], LANES), lambda i, nb=nb: (0, jnp.minimum(i, nb - 1))))
        out_shapes.append(jax.ShapeDtypeStruct(w.shape[::-1], BF16))
    tile = pl.BlockSpec((TM_FFN, D_MODEL), lambda i: (i, 0))
    outs = pl.pallas_call(
        functools.partial(_ffn_kernel, final_norm=final_norm, n_side=len(in_side),
                          n_side_t=len(side_t)),
        grid=(steps,),
        in_specs=[tile, _resident((1, D_MODEL)), _resident((D_MODEL, D_FF)),
                  _resident((D_MODEL, D_FF)), _resident((D_FF, D_MODEL))]
                 + [_resident((1, D_MODEL))] * final_norm + in_side,
        out_specs=[tile] + out_side,
        out_shape=[jax.ShapeDtypeStruct((n, D_MODEL), F32)] + out_shapes,
        compiler_params=pltpu.CompilerParams(
            dimension_semantics=("arbitrary",), vmem_limit_bytes=VMEM_LIMIT_BYTES),
        name="ffn_final" if final_norm else "ffn",
    )(x2d, norm_g, wg, wu, wd, *([final_g] if final_norm else []), *side, *side_t)
    return outs[0], outs[1:]


def _head_rms(t, gain, bd):
    sq = t * t
    hi = sq.astype(BF16)
    lo = (sq - hi.astype(F32)).astype(BF16)
    parts = []
    for s in range(D_ATTN // MXU_DIM):
        sl = slice(s * MXU_DIM, (s + 1) * MXU_DIM)
        parts.append(_dot(hi[:, sl], bd) + _dot(lo[:, sl], bd))
    ms = jnp.concatenate(parts, axis=1)
    return t * lax.rsqrt(ms + EPS) * gain


def _mixer_kernel(x_ref, g_ref, wk_ref, wu_ref, wqt_ref, wvt_ref, qg_ref, kg_ref, bd_ref, tblt_ref,
                  pw_ref, ps_ref, wouta_ref, woutp_ref, o_ref,
                  qt_scr, k_scr, vt_scr, halo_scr, d_scr, catt_scr, pt_scr):
    step = pl.program_id(0)
    j = step % TILES_PER_SEQ

    @pl.when(step == 0)
    def _():
        k_scr[...] = jnp.zeros(k_scr.shape, BF16)
        vt_scr[...] = jnp.zeros(vt_scr.shape, BF16)
        halo_scr[...] = jnp.zeros(halo_scr.shape, F32)
        pt_scr[...] = jnp.zeros(pt_scr.shape, BF16)

    parity = step % 2

    x = x_ref[...]
    hf = _rms(x, g_ref[...])
    h = hf.astype(BF16)
    ht = hf.T.astype(BF16)
    k_scr[parity] = _head_rms(_dot(h, wk_ref[...]), kg_ref[...], bd_ref[...]).astype(BF16)
    u = _dot(h, wu_ref[...])
    qt = _dot(wqt_ref[...], ht)
    for head in range(N_HEADS):
        rows = slice(head * HEAD_DIM, (head + 1) * HEAD_DIM)
        qh = qt[rows, :]
        ms = jnp.mean(qh * qh, axis=0, keepdims=True)
        qt_scr[rows, :] = (qh * lax.rsqrt(ms + EPS) * qg_ref[rows, :]).astype(BF16)
    vt_scr[parity] = _dot(wvt_ref[...], ht).astype(BF16)
    halo = jnp.where(jnp.full((POOL_HALO, D_POOL), j, jnp.int32) == 0, 0.0, halo_scr[1 - parity])
    halo_scr[parity] = u[TM_MIX - POOL_HALO:, :]

    feat = lax.broadcasted_iota(jnp.int32, (2 * HEAD_DIM, Q_BLOCK), 0)
    pen = jnp.where(jnp.full((1, LANES), j, jnp.int32) == 0, NEG_INF, 0.0)

    def pair_scores(blk, pair):
        r0 = blk * Q_BLOCK
        n_prev = TM_MIX - r0
        ls = slice(pair * LANES, (pair + 1) * LANES)
        qt2 = qt_scr[ls, r0:r0 + Q_BLOCK]
        zero = jnp.zeros_like(qt2)
        even = jnp.where(feat < HEAD_DIM, qt2, zero)
        odd = jnp.where(feat >= HEAD_DIM, qt2, zero)
        out = []
        for half in range(Q_BLOCK // LANES):
            cl = slice(half * LANES, (half + 1) * LANES)
            qmt = jnp.concatenate([even[:, cl], odd[:, cl]], axis=1)
            lo = half * KEY_GROUP
            hi = lo + BAND_GROUPS * KEY_GROUP
            out.append((_dot(k_scr[1 - parity, r0 + lo:, ls], qmt),
                        _dot(k_scr[parity, 0:hi - n_prev, ls], qmt)))
        return out

    units = [(blk, pair) for blk in range(TM_MIX // Q_BLOCK) for pair in range(N_HEADS // 2)]
    ahead = pair_scores(*units[0])
    for n, (blk, pair) in enumerate(units):
        st = ahead
        if n + 1 < len(units):
            ahead = pair_scores(*units[n + 1])
        r0 = blk * Q_BLOCK
        n_prev = TM_MIX - r0
        qcols = slice(r0, r0 + Q_BLOCK)
        for hh in range(2):
            head = 2 * pair + hh
            p_buf = pt_scr.at[2 * n + hh]
            sums = []
            for half in range(Q_BLOCK // LANES):
                cl = slice(half * LANES, (half + 1) * LANES)
                sl = slice(hh * LANES, (hh + 1) * LANES)
                st_prev, st_cur = st[half]
                lo = half * KEY_GROUP

                def scores(grp):
                    w0 = grp * KEY_GROUP
                    if w0 < n_prev:
                        blk_s = st_prev[w0 - lo:w0 - lo + KEY_GROUP, sl]
                    else:
                        blk_s = st_cur[w0 - n_prev:w0 - n_prev + KEY_GROUP, sl]
                    if _TABLE_NEEDED[grp][half]:
                        blk_s = blk_s + tblt_ref[head, w0:w0 + KEY_GROUP, cl]
                    return blk_s

                def group_pen(grp):
                    return pen if grp * KEY_GROUP < n_prev else 0.0

                groups = range(half, half + BAND_GROUPS)
                m = functools.reduce(jnp.maximum, [
                    jnp.max(scores(grp), axis=0, keepdims=True) + group_pen(grp) for grp in groups])
                total = None
                for grp in groups:
                    e = jnp.exp2(scores(grp) - (m - group_pen(grp)))
                    part = jnp.sum(e, axis=0, keepdims=True)
                    total = part if total is None else total + part
                    p_buf[grp * KEY_GROUP:(grp + 1) * KEY_GROUP, cl] = e.astype(BF16)
                sums.append(total)
            hrows = slice(head * HEAD_DIM, (head + 1) * HEAD_DIM)
            ot = (_dot(vt_scr[1 - parity, hrows, r0:], p_buf[0:n_prev, :])
                  + _dot(vt_scr[parity, hrows, 0:WINDOW - n_prev], p_buf[n_prev:, :]))
            catt_scr[hrows, qcols] = (ot / jnp.concatenate(sums, axis=1)).astype(BF16)

    pos = lax.broadcasted_iota(jnp.int32, (TM_MIX, 1), 0) + j * TM_MIX
    for g, w in enumerate(POOL_WINDOWS):
        ls = slice(g * POOL_GROUP_DIM, (g + 1) * POOL_GROUP_DIM)
        e = jnp.concatenate([halo[:, ls], u[:, ls]], axis=0)
        acc = e
        sh = 1
        while sh < w:
            acc = acc + pltpu.roll(acc, sh, axis=0)
            sh *= 2
        cnt = jnp.minimum(pos + 1, w).astype(F32)
        d = acc[POOL_HALO:, :] / cnt - u[:, ls]
        d_scr[:, ls] = d.astype(BF16)
    pooled = []
    for pr in range(D_POOL // MXU_DIM):
        sl = slice(pr * MXU_DIM, (pr + 1) * MXU_DIM)
        pooled.append((_dot(d_scr[:, sl], pw_ref[pr]) * ps_ref[:, sl]).astype(BF16))

    attn = lax.dot_general(catt_scr[...], wouta_ref[...], (((0,), (0,)), ((), ())),
                           preferred_element_type=F32)
    o_ref[...] = x + attn + _dot(jnp.concatenate(pooled, axis=1), woutp_ref[...])


def _mixer(x2d, norm_g, w_in_b, w_in_t, qg, kg, bd, tblt, pw_bd, ps, w_out_b):
    n = x2d.shape[0]

    def part(shape, *index):
        return pl.BlockSpec(shape, lambda i: index, pipeline_mode=pl.Buffered(1))

    return pl.pallas_call(
        _mixer_kernel,
        grid=(n // TM_MIX,),
        in_specs=[
            pl.BlockSpec((TM_MIX, D_MODEL), lambda i: (i, 0)),
            _resident((1, D_MODEL)),
            part((D_MODEL, D_ATTN), 0, 1),
            part((D_MODEL, D_POOL), 0, 3),
            part((D_ATTN, D_MODEL), 0, 0),
            part((D_ATTN, D_MODEL), 2, 0),
            _resident((D_ATTN, 1)),
            _resident((1, D_ATTN)),
            _resident((MXU_DIM, MXU_DIM)),
            _resident((N_HEADS, WINDOW, Q_BLOCK)),
            _resident((D_POOL // MXU_DIM, MXU_DIM, MXU_DIM)),
            _resident((1, D_POOL)),
            part((D_ATTN, D_MODEL), 0, 0),
            part((D_POOL, D_MODEL), 1, 0),
        ],
        out_specs=pl.BlockSpec((TM_MIX, D_MODEL), lambda i: (i, 0)),
        out_shape=jax.ShapeDtypeStruct((n, D_MODEL), F32),
        scratch_shapes=[
            pltpu.VMEM((D_ATTN, TM_MIX), BF16),
            pltpu.VMEM((2, TM_MIX, D_ATTN), BF16),
            pltpu.VMEM((2, D_ATTN, TM_MIX), BF16),
            pltpu.VMEM((2, POOL_HALO, D_POOL), F32),
            pltpu.VMEM((TM_MIX, D_POOL), BF16),
            pltpu.VMEM((D_ATTN, TM_MIX), BF16),
            pltpu.VMEM((UNITS, WINDOW, Q_BLOCK), BF16),
        ],
        compiler_params=pltpu.CompilerParams(
            dimension_semantics=("arbitrary",), vmem_limit_bytes=VMEM_LIMIT_BYTES),
        name="mixer",
    )(x2d, norm_g, w_in_b, w_in_b, w_in_t, w_in_t, qg, kg, bd, tblt, pw_bd, ps, w_out_b, w_out_b)


def _band_masks():
    w = np.arange(WINDOW)[:, None]
    q = np.arange(Q_BLOCK)[None, :]
    band = w - (q // CHUNK) * CHUNK
    in_band = (band >= 0) & (band < SPAN + CHUNK)
    far = in_band & (w - q <= SPAN - REL_CLIP)
    return in_band, far


def _table_needed():
    in_band, far = _band_masks()
    needed = []
    for grp in range(WINDOW // KEY_GROUP):
        rows = slice(grp * KEY_GROUP, (grp + 1) * KEY_GROUP)
        row = []
        for half in range(Q_BLOCK // LANES):
            cols = slice(half * LANES, (half + 1) * LANES)
            inside = half <= grp < half + BAND_GROUPS
            assert inside or not in_band[rows, cols].any()
            row.append(bool(inside and not far[rows, cols].all()))
        needed.append(row)
    return needed


_TABLE_NEEDED = _table_needed()


def _bias_table_t(rel_bias):
    period = 1024
    band_len = SPAN + CHUNK
    n_far = SPAN - REL_CLIP + 1
    n_near = REL_CLIP + CHUNK - 1
    rel = (rel_bias - rel_bias[:, 2 * REL_CLIP:]).astype(F32) * LOG2E
    base = jnp.concatenate([
        jnp.zeros((N_HEADS, n_far), F32),
        rel[:, 2 * REL_CLIP - 1:2 * REL_CLIP - 1 - n_near:-1],
        jnp.zeros((N_HEADS, period - n_far - n_near), F32),
    ], axis=1)
    flat = jnp.tile(base, (1, CHUNK))[:, :CHUNK * (period - 1)]
    chunk_t = flat.reshape(N_HEADS, CHUNK, period - 1)[:, :, :band_len].transpose(0, 2, 1)
    cols = [jnp.pad(chunk_t, ((0, 0), (cq * CHUNK, WINDOW - band_len - cq * CHUNK), (0, 0)),
                    constant_values=NEG_INF) for cq in range(Q_BLOCK // CHUNK)]
    return jnp.concatenate(cols, axis=2)


def _block_diag2(a, b):
    z = jnp.zeros_like(a)
    return jnp.concatenate([jnp.concatenate([a, z], axis=1),
                            jnp.concatenate([z, b], axis=1)], axis=0)


def kernel(x, ffn1_norm, ffn1_w_gate, ffn1_w_up, ffn1_w_down, mix_norm, w_in, q_norm, k_norm,
           rel_bias, pool_w, pool_scale, w_out, ffn2_norm, ffn2_w_gate, ffn2_w_up, ffn2_w_down,
           final_norm):
    b, s, d = x.shape
    assert (s, d) == (SEQ, D_MODEL) and ffn1_norm.shape[0] == 1
    x2d = x.reshape(b * s, d)
    l = 0
    head_id = jnp.arange(MXU_DIM) // HEAD_DIM
    bd = jnp.where(head_id[:, None] == head_id[None, :], 1.0 / HEAD_DIM, 0.0).astype(BF16)
    qg = (jnp.tile(q_norm[l], N_HEADS) * (HEAD_DIM ** -0.5 * LOG2E))[:, None]
    kg = jnp.tile(k_norm[l], N_HEADS)[None, :]
    pw = pool_w[l].astype(BF16)
    pw_bd = jnp.stack([_block_diag2(pw[0], pw[1]), _block_diag2(pw[2], pw[3])])
    x2d, (w_in_b, w_out_b, wg2, wu2, wd2, w_in_t) = _ffn(
        x2d, ffn1_norm[l][None, :], ffn1_w_gate[l].astype(BF16), ffn1_w_up[l].astype(BF16),
        ffn1_w_down[l].astype(BF16),
        side=(w_in[l], w_out[l], ffn2_w_gate[l], ffn2_w_up[l], ffn2_w_down[l]), side_t=(w_in[l],))
    x2d = _mixer(x2d, mix_norm[l][None, :], w_in_b, w_in_t, qg, kg, bd, _bias_table_t(rel_bias[l]),
                 pw_bd, pool_scale[l][None, :], w_out_b)
    x2d, _ = _ffn(x2d, ffn2_norm[l][None, :], wg2, wu2, wd2, final_g=final_norm[l][None, :])
    return x2d.reshape(b, s, d)
```

```python
import functools
import math

import numpy as np
import jax
import jax.numpy as jnp
from jax import lax
from jax.experimental import pallas as pl
from jax.experimental.pallas import tpu as pltpu

D_MODEL = 1024
SEQ = 4096
CHUNK = 64
N_LEFT_CHUNKS = 8
D_ATTN = 512
HEAD_DIM = 64
N_HEADS = 8
D_POOL = 512
POOL_WINDOWS = (2, 4, 8, 16)
POOL_GROUP_DIM = 128
REL_CLIP = 128
D_FF = 2816
D_IN = 3 * D_ATTN + D_POOL
EPS = 1e-6
NEG_INF = -1e30

LANES = 128
BF16_SUBLANES = 16
MXU_DIM = 256
VMEM_LIMIT_BYTES = 56 * 1024 * 1024

TM_FFN = 1024
TM_MIX = 512
TILES_PER_SEQ = SEQ // TM_MIX
Q_BLOCK = 256
SPAN = N_LEFT_CHUNKS * CHUNK
WINDOW = Q_BLOCK + SPAN
KEY_GROUP = 128
BAND_GROUPS = (SPAN + 2 * CHUNK) // KEY_GROUP
UNITS = N_HEADS * TM_MIX // Q_BLOCK
POOL_HALO = 16
LOG2E = math.log2(math.e)

BF16 = jnp.bfloat16
F32 = jnp.float32


def _rms(x, gain):
    return x * lax.rsqrt(jnp.mean(x * x, axis=-1, keepdims=True) + EPS) * gain


def _dot(a, b):
    return jnp.dot(a, b, preferred_element_type=F32)


def _ffn_kernel(*refs, final_norm, n_side, n_side_t):
    n_in = 5 + int(final_norm)
    x_ref, g_ref, wg_ref, wu_ref, wd_ref = refs[:5]
    side_in = refs[n_in:n_in + n_side]
    o_ref = refs[n_in + n_side]
    side_out = refs[n_in + n_side + 1:]
    x = x_ref[...]
    h = _rms(x, g_ref[...]).astype(BF16)
    gate = _dot(h, wg_ref[...])
    up = _dot(h, wu_ref[...])
    act = (gate * jax.nn.sigmoid(gate) * up).astype(BF16)
    y = x + 0.5 * _dot(act, wd_ref[...])
    if final_norm:
        y = _rms(y, refs[5][...])
    o_ref[...] = y
    for k, (src, dst) in enumerate(zip(side_in, side_out)):
        w = src[...]
        dst[...] = (w.T if k >= n_side - n_side_t else w).astype(BF16)


def _resident(shape):
    return pl.BlockSpec(shape, lambda i: (0,) * len(shape), pipeline_mode=pl.Buffered(1))


def _row_blocks(rows, steps):
    return max(nb for nb in range(1, steps + 1)
               if rows % nb == 0 and (rows // nb) % BF16_SUBLANES == 0)


def _ffn(x2d, norm_g, wg, wu, wd, final_g=None, side=(), side_t=()):
    n = x2d.shape[0]
    steps = n // TM_FFN
    final_norm = final_g is not None
    in_side, out_side, out_shapes = [], [], []
    for w in side:
        nb = _row_blocks(w.shape[0], steps)
        spec = pl.BlockSpec(
            (w.shape[0] // nb, w.shape[1]), lambda i, nb=nb: (jnp.minimum(i, nb - 1), 0))
        in_side.append(spec)
        out_side.append(spec)
        out_shapes.append(jax.ShapeDtypeStruct(w.shape, BF16))
    for w in side_t:
        nb = w.shape[0] // LANES
        in_side.append(pl.BlockSpec(
            (LANES, w.shape[1]), lambda i, nb=nb: (jnp.minimum(i, nb - 1), 0)))
        out_side.append(pl.BlockSpec(
            (w.shape[1], LANES), lambda i, nb=nb: (0, jnp.minimum(i, nb - 1))))
        out_shapes.append(jax.ShapeDtypeStruct(w.shape[::-1], BF16))
    tile = pl.BlockSpec((TM_FFN, D_MODEL), lambda i: (i, 0))
    outs = pl.pallas_call(
        functools.partial(_ffn_kernel, final_norm=final_norm, n_side=len(in_side),
                          n_side_t=len(side_t)),
        grid=(steps,),
        in_specs=[tile, _resident((1, D_MODEL)), _resident((D_MODEL, D_FF)),
                  _resident((D_MODEL, D_FF)), _resident((D_FF, D_MODEL))]
                 + [_resident((1, D_MODEL))] * final_norm + in_side,
        out_specs=[tile] + out_side,
        out_shape=[jax.ShapeDtypeStruct((n, D_MODEL), F32)] + out_shapes,
        compiler_params=pltpu.CompilerParams(
            dimension_semantics=("arbitrary",), vmem_limit_bytes=VMEM_LIMIT_BYTES),
        name="ffn_final" if final_norm else "ffn",
    )(x2d, norm_g, wg, wu, wd, *([final_g] if final_norm else []), *side, *side_t)
    return outs[0], outs[1:]


def _head_rms(t, gain, bd):
    sq = t * t
    hi = sq.astype(BF16)
    lo = (sq - hi.astype(F32)).astype(BF16)
    parts = []
    for s in range(D_ATTN // MXU_DIM):
        sl = slice(s * MXU_DIM, (s + 1) * MXU_DIM)
        parts.append(_dot(hi[:, sl], bd) + _dot(lo[:, sl], bd))
    ms = jnp.concatenate(parts, axis=1)
    return t * lax.rsqrt(ms + EPS) * gain


def _mixer_kernel(x_ref, g_ref, wk_ref, wu_ref, wqt_ref, wvt_ref, qg_ref, kg_ref, bd_ref, tblt_ref,
                  pw_ref, ps_ref, wouta_ref, woutp_ref, o_ref,
                  qt_scr, k_scr, vt_scr, halo_scr, d_scr, catt_scr, pt_scr):
    step = pl.program_id(0)
    j = step % TILES_PER_SEQ

    @pl.when(step == 0)
    def _():
        k_scr[...] = jnp.zeros(k_scr.shape, BF16)
        vt_scr[...] = jnp.zeros(vt_scr.shape, BF16)
        halo_scr[...] = jnp.zeros(halo_scr.shape, F32)
        pt_scr[...] = jnp.zeros(pt_scr.shape, BF16)

    parity = step % 2

    x = x_ref[...]
    hf = _rms(x, g_ref[...])
    h = hf.astype(BF16)
    ht = h.T
    k_scr[parity] = _head_rms(_dot(h, wk_ref[...]), kg_ref[...], bd_ref[...]).astype(BF16)
    u = _dot(h, wu_ref[...])
    qt = _dot(wqt_ref[...], ht)
    for head in range(N_HEADS):
        rows = slice(head * HEAD_DIM, (head + 1) * HEAD_DIM)
        qh = qt[rows, :]
        ms = jnp.mean(qh * qh, axis=0, keepdims=True)
        qt_scr[rows, :] = (qh * lax.rsqrt(ms + EPS) * qg_ref[rows, :]).astype(BF16)
    vt_scr[parity] = _dot(wvt_ref[...], ht).astype(BF16)
    halo = jnp.where(jnp.full((POOL_HALO, D_POOL), j, jnp.int32) == 0, 0.0, halo_scr[1 - parity])
    halo_scr[parity] = u[TM_MIX - POOL_HALO:, :]

    feat = lax.broadcasted_iota(jnp.int32, (2 * HEAD_DIM, Q_BLOCK), 0)
    pen = jnp.where(jnp.full((1, LANES), j, jnp.int32) == 0, NEG_INF, 0.0)

    def pair_scores(blk, pair):
        r0 = blk * Q_BLOCK
        n_prev = TM_MIX - r0
        ls = slice(pair * LANES, (pair + 1) * LANES)
        qt2 = qt_scr[ls, r0:r0 + Q_BLOCK]
        zero = jnp.zeros_like(qt2)
        even = jnp.where(feat < HEAD_DIM, qt2, zero)
        odd = jnp.where(feat >= HEAD_DIM, qt2, zero)
        out = []
        for half in range(Q_BLOCK // LANES):
            cl = slice(half * LANES, (half + 1) * LANES)
            qmt = jnp.concatenate([even[:, cl], odd[:, cl]], axis=1)
            lo = half * KEY_GROUP
            hi = lo + BAND_GROUPS * KEY_GROUP
            out.append((_dot(k_scr[1 - parity, r0 + lo:, ls], qmt),
                        _dot(k_scr[parity, 0:hi - n_prev, ls], qmt)))
        return out

    units = [(blk, pair) for blk in range(TM_MIX // Q_BLOCK) for pair in range(N_HEADS // 2)]
    ahead = pair_scores(*units[0])
    for n, (blk, pair) in enumerate(units):
        st = ahead
        if n + 1 < len(units):
            ahead = pair_scores(*units[n + 1])
        r0 = blk * Q_BLOCK
        n_prev = TM_MIX - r0
        qcols = slice(r0, r0 + Q_BLOCK)
        for hh in range(2):
            head = 2 * pair + hh
            p_buf = pt_scr.at[2 * n + hh]
            sums = []
            for half in range(Q_BLOCK // LANES):
                cl = slice(half * LANES, (half + 1) * LANES)
                sl = slice(hh * LANES, (hh + 1) * LANES)
                st_prev, st_cur = st[half]
                lo = half * KEY_GROUP

                def scores(grp):
                    w0 = grp * KEY_GROUP
                    if w0 < n_prev:
                        blk_s = st_prev[w0 - lo:w0 - lo + KEY_GROUP, sl]
                    else:
                        blk_s = st_cur[w0 - n_prev:w0 - n_prev + KEY_GROUP, sl]
                    if _TABLE_NEEDED[grp][half]:
                        blk_s = blk_s + tblt_ref[head, w0:w0 + KEY_GROUP, cl]
                    return blk_s

                def group_pen(grp):
                    return pen if grp * KEY_GROUP < n_prev else 0.0

                groups = range(half, half + BAND_GROUPS)
                m = functools.reduce(jnp.maximum, [
                    jnp.max(scores(grp), axis=0, keepdims=True) + group_pen(grp) for grp in groups])
                total = None
                for grp in groups:
                    e = jnp.exp2(scores(grp) - (m - group_pen(grp)))
                    part = jnp.sum(e, axis=0, keepdims=True)
                    total = part if total is None else total + part
                    p_buf[grp * KEY_GROUP:(grp + 1) * KEY_GROUP, cl] = e.astype(BF16)
                sums.append(total)
            hrows = slice(head * HEAD_DIM, (head + 1) * HEAD_DIM)
            ot = (_dot(vt_scr[1 - parity, hrows, r0:], p_buf[0:n_prev, :])
                  + _dot(vt_scr[parity, hrows, 0:WINDOW - n_prev], p_buf[n_prev:, :]))
            catt_scr[hrows, qcols] = (ot / jnp.concatenate(sums, axis=1)).astype(BF16)

    pos = lax.broadcasted_iota(jnp.int32, (TM_MIX, 1), 0) + j * TM_MIX
    for g, w in enumerate(POOL_WINDOWS):
        ls = slice(g * POOL_GROUP_DIM, (g + 1) * POOL_GROUP_DIM)
        e = jnp.concatenate([halo[:, ls], u[:, ls]], axis=0)
        acc = e
        sh = 1
        while sh < w:
            acc = acc + pltpu.roll(acc, sh, axis=0)
            sh *= 2
        cnt = jnp.minimum(pos + 1, w).astype(F32)
        d = acc[POOL_HALO:, :] / cnt - u[:, ls]
        d_scr[:, ls] = d.astype(BF16)
    pooled = []
    for pr in range(D_POOL // MXU_DIM):
        sl = slice(pr * MXU_DIM, (pr + 1) * MXU_DIM)
        pooled.append((_dot(d_scr[:, sl], pw_ref[pr]) * ps_ref[:, sl]).astype(BF16))

    mixed = jnp.concatenate([catt_scr[...].T] + pooled, axis=1)
    o_ref[...] = x + _dot(mixed, jnp.concatenate([wouta_ref[...], woutp_ref[...]], axis=0))


def _mixer(x2d, norm_g, w_in_b, w_in_t, qg, kg, bd, tblt, pw_bd, ps, w_out_b):
    n = x2d.shape[0]

    def part(shape, *index):
        return pl.BlockSpec(shape, lambda i: index, pipeline_mode=pl.Buffered(1))

    return pl.pallas_call(
        _mixer_kernel,
        grid=(n // TM_MIX,),
        in_specs=[
            pl.BlockSpec((TM_MIX, D_MODEL), lambda i: (i, 0)),
            _resident((1, D_MODEL)),
            part((D_MODEL, D_ATTN), 0, 1),
            part((D_MODEL, D_POOL), 0, 3),
            part((D_ATTN, D_MODEL), 0, 0),
            part((D_ATTN, D_MODEL), 2, 0),
            _resident((D_ATTN, 1)),
            _resident((1, D_ATTN)),
            _resident((MXU_DIM, MXU_DIM)),
            _resident((N_HEADS, WINDOW, Q_BLOCK)),
            _resident((D_POOL // MXU_DIM, MXU_DIM, MXU_DIM)),
            _resident((1, D_POOL)),
            part((D_ATTN, D_MODEL), 0, 0),
            part((D_POOL, D_MODEL), 1, 0),
        ],
        out_specs=pl.BlockSpec((TM_MIX, D_MODEL), lambda i: (i, 0)),
        out_shape=jax.ShapeDtypeStruct((n, D_MODEL), F32),
        scratch_shapes=[
            pltpu.VMEM((D_ATTN, TM_MIX), BF16),
            pltpu.VMEM((2, TM_MIX, D_ATTN), BF16),
            pltpu.VMEM((2, D_ATTN, TM_MIX), BF16),
            pltpu.VMEM((2, POOL_HALO, D_POOL), F32),
            pltpu.VMEM((TM_MIX, D_POOL), BF16),
            pltpu.VMEM((D_ATTN, TM_MIX), BF16),
            pltpu.VMEM((UNITS, WINDOW, Q_BLOCK), BF16),
        ],
        compiler_params=pltpu.CompilerParams(
            dimension_semantics=("arbitrary",), vmem_limit_bytes=VMEM_LIMIT_BYTES),
        name="mixer",
    )(x2d, norm_g, w_in_b, w_in_b, w_in_t, w_in_t, qg, kg, bd, tblt, pw_bd, ps, w_out_b, w_out_b)


def _band_masks():
    w = np.arange(WINDOW)[:, None]
    q = np.arange(Q_BLOCK)[None, :]
    band = w - (q // CHUNK) * CHUNK
    in_band = (band >= 0) & (band < SPAN + CHUNK)
    far = in_band & (w - q <= SPAN - REL_CLIP)
    return in_band, far


def _table_needed():
    in_band, far = _band_masks()
    needed = []
    for grp in range(WINDOW // KEY_GROUP):
        rows = slice(grp * KEY_GROUP, (grp + 1) * KEY_GROUP)
        row = []
        for half in range(Q_BLOCK // LANES):
            cols = slice(half * LANES, (half + 1) * LANES)
            inside = half <= grp < half + BAND_GROUPS
            assert inside or not in_band[rows, cols].any()
            row.append(bool(inside and not far[rows, cols].all()))
        needed.append(row)
    return needed


_TABLE_NEEDED = _table_needed()


def _bias_table_t(rel_bias):
    period = 1024
    band_len = SPAN + CHUNK
    n_far = SPAN - REL_CLIP + 1
    n_near = REL_CLIP + CHUNK - 1
    rel = (rel_bias - rel_bias[:, 2 * REL_CLIP:]).astype(F32) * LOG2E
    base = jnp.concatenate([
        jnp.zeros((N_HEADS, n_far), F32),
        rel[:, 2 * REL_CLIP - 1:2 * REL_CLIP - 1 - n_near:-1],
        jnp.zeros((N_HEADS, period - n_far - n_near), F32),
    ], axis=1)
    flat = jnp.tile(base, (1, CHUNK))[:, :CHUNK * (period - 1)]
    chunk_t = flat.reshape(N_HEADS, CHUNK, period - 1)[:, :, :band_len].transpose(0, 2, 1)
    cols = [jnp.pad(chunk_t, ((0, 0), (cq * CHUNK, WINDOW - band_len - cq * CHUNK), (0, 0)),
                    constant_values=NEG_INF) for cq in range(Q_BLOCK // CHUNK)]
    return jnp.concatenate(cols, axis=2)


def _block_diag2(a, b):
    z = jnp.zeros_like(a)
    return jnp.concatenate([jnp.concatenate([a, z], axis=1),
                            jnp.concatenate([z, b], axis=1)], axis=0)


def kernel(x, ffn1_norm, ffn1_w_gate, ffn1_w_up, ffn1_w_down, mix_norm, w_in, q_norm, k_norm,
           rel_bias, pool_w, pool_scale, w_out, ffn2_norm, ffn2_w_gate, ffn2_w_up, ffn2_w_down,
           final_norm):
    b, s, d = x.shape
    assert (s, d) == (SEQ, D_MODEL) and ffn1_norm.shape[0] == 1
    x2d = x.reshape(b * s, d)
    l = 0
    head_id = jnp.arange(MXU_DIM) // HEAD_DIM
    bd = jnp.where(head_id[:, None] == head_id[None, :], 1.0 / HEAD_DIM, 0.0).astype(BF16)
    qg = (jnp.tile(q_norm[l], N_HEADS) * (HEAD_DIM ** -0.5 * LOG2E))[:, None]
    kg = jnp.tile(k_norm[l], N_HEADS)[None, :]
    pw = pool_w[l].astype(BF16)
    pw_bd = jnp.stack([_block_diag2(pw[0], pw[1]), _block_diag2(pw[2], pw[3])])
    x2d, (w_in_b, w_out_b, wg2, wu2, wd2, w_in_t) = _ffn(
        x2d, ffn1_norm[l][None, :], ffn1_w_gate[l].astype(BF16), ffn1_w_up[l].astype(BF16),
        ffn1_w_down[l].astype(BF16),
        side=(w_in[l], w_out[l], ffn2_w_gate[l], ffn2_w_up[l], ffn2_w_down[l]), side_t=(w_in[l],))
    x2d = _mixer(x2d, mix_norm[l][None, :], w_in_b, w_in_t, qg, kg, bd, _bias_table_t(rel_bias[l]),
                 pw_bd, pool_scale[l][None, :], w_out_b)
    x2d, _ = _ffn(x2d, ffn2_norm[l][None, :], wg2, wu2, wd2, final_g=final_norm[l][None, :])
    return x2d.reshape(b, s, d)
```

```python
import functools
import math

import numpy as np
import jax
import jax.numpy as jnp
from jax import lax
from jax.experimental import pallas as pl
from jax.experimental.pallas import tpu as pltpu

D_MODEL = 1024
SEQ = 4096
CHUNK = 64
N_LEFT_CHUNKS = 8
D_ATTN = 512
HEAD_DIM = 64
N_HEADS = 8
D_POOL = 512
POOL_WINDOWS = (2, 4, 8, 16)
POOL_GROUP_DIM = 128
REL_CLIP = 128
D_FF = 2816
D_IN = 3 * D_ATTN + D_POOL
EPS = 1e-6
NEG_INF = -1e30

LANES = 128
BF16_SUBLANES = 16
MXU_DIM = 256
VMEM_LIMIT_BYTES = 56 * 1024 * 1024

TM_FFN = 1024
TM_MIX = 512
TILES_PER_SEQ = SEQ // TM_MIX
Q_BLOCK = 256
SPAN = N_LEFT_CHUNKS * CHUNK
WINDOW = Q_BLOCK + SPAN
KEY_GROUP = 128
BAND_GROUPS = (SPAN + 2 * CHUNK) // KEY_GROUP
UNITS = N_HEADS * TM_MIX // Q_BLOCK
POOL_HALO = 16
LOG2E = math.log2(math.e)

BF16 = jnp.bfloat16
F32 = jnp.float32


def _rms(x, gain):
    return x * lax.rsqrt(jnp.mean(x * x, axis=-1, keepdims=True) + EPS) * gain


def _dot(a, b):
    return jnp.dot(a, b, preferred_element_type=F32)


def _ffn_kernel(*refs, final_norm, n_side, n_side_t):
    n_in = 5 + int(final_norm)
    x_ref, g_ref, wg_ref, wu_ref, wd_ref = refs[:5]
    side_in = refs[n_in:n_in + n_side]
    o_ref = refs[n_in + n_side]
    side_out = refs[n_in + n_side + 1:]
    x = x_ref[...]
    h = _rms(x, g_ref[...]).astype(BF16)
    gate = _dot(h, wg_ref[...])
    up = _dot(h, wu_ref[...])
    act = (gate * jax.nn.sigmoid(gate) * up).astype(BF16)
    y = x + 0.5 * _dot(act, wd_ref[...])
    if final_norm:
        y = _rms(y, refs[5][...])
    o_ref[...] = y
    for k, (src, dst) in enumerate(zip(side_in, side_out)):
        w = src[...]
        dst[...] = (w.T if k >= n_side - n_side_t else w).astype(BF16)


def _resident(shape):
    return pl.BlockSpec(shape, lambda i: (0,) * len(shape), pipeline_mode=pl.Buffered(1))


def _row_blocks(rows, steps):
    return max(nb for nb in range(1, steps + 1)
               if rows % nb == 0 and (rows // nb) % BF16_SUBLANES == 0)


def _ffn(x2d, norm_g, wg, wu, wd, final_g=None, side=(), side_t=()):
    n = x2d.shape[0]
    steps = n // TM_FFN
    final_norm = final_g is not None
    in_side, out_side, out_shapes = [], [], []
    for w in side:
        nb = _row_blocks(w.shape[0], steps)
        spec = pl.BlockSpec(
            (w.shape[0] // nb, w.shape[1]), lambda i, nb=nb: (jnp.minimum(i, nb - 1), 0))
        in_side.append(spec)
        out_side.append(spec)
        out_shapes.append(jax.ShapeDtypeStruct(w.shape, BF16))
    for w in side_t:
        nb = w.shape[0] // LANES
        in_side.append(pl.BlockSpec(
            (LANES, w.shape[1]), lambda i, nb=nb: (jnp.minimum(i, nb - 1), 0)))
        out_side.append(pl.BlockSpec(
            (w.shape[1], LANES), lambda i, nb=nb: (0, jnp.minimum(i, nb - 1))))
        out_shapes.append(jax.ShapeDtypeStruct(w.shape[::-1], BF16))
    tile = pl.BlockSpec((TM_FFN, D_MODEL), lambda i: (i, 0))
    outs = pl.pallas_call(
        functools.partial(_ffn_kernel, final_norm=final_norm, n_side=len(in_side),
                          n_side_t=len(side_t)),
        grid=(steps,),
        in_specs=[tile, _resident((1, D_MODEL)), _resident((D_MODEL, D_FF)),
                  _resident((D_MODEL, D_FF)), _resident((D_FF, D_MODEL))]
                 + [_resident((1, D_MODEL))] * final_norm + in_side,
        out_specs=[tile] + out_side,
        out_shape=[jax.ShapeDtypeStruct((n, D_MODEL), F32)] + out_shapes,
        compiler_params=pltpu.CompilerParams(
            dimension_semantics=("arbitrary",), vmem_limit_bytes=VMEM_LIMIT_BYTES),
        name="ffn_final" if final_norm else "ffn",
    )(x2d, norm_g, wg, wu, wd, *([final_g] if final_norm else []), *side, *side_t)
    return outs[0], outs[1:]


def _head_rms(t, gain, bd):
    sq = t * t
    hi = sq.astype(BF16)
    lo = (sq - hi.astype(F32)).astype(BF16)
    parts = []
    for s in range(D_ATTN // MXU_DIM):
        sl = slice(s * MXU_DIM, (s + 1) * MXU_DIM)
        parts.append(_dot(hi[:, sl], bd) + _dot(lo[:, sl], bd))
    ms = jnp.concatenate(parts, axis=1)
    return t * lax.rsqrt(ms + EPS) * gain


def _mixer_kernel(x_ref, g_ref, wk_ref, wu_ref, wqt_ref, wvt_ref, qg_ref, kg_ref, bd_ref, tblt_ref,
                  pw_ref, ps_ref, wouta_ref, woutp_ref, o_ref,
                  qt_scr, k_scr, vt_scr, halo_scr, d_scr, catt_scr, pt_scr):
    step = pl.program_id(0)
    j = step % TILES_PER_SEQ

    @pl.when(step == 0)
    def _():
        k_scr[...] = jnp.zeros(k_scr.shape, BF16)
        vt_scr[...] = jnp.zeros(vt_scr.shape, BF16)
        halo_scr[...] = jnp.zeros(halo_scr.shape, F32)
        pt_scr[...] = jnp.zeros(pt_scr.shape, BF16)

    parity = step % 2

    x = x_ref[...]
    hf = _rms(x, g_ref[...])
    h = hf.astype(BF16)
    ht = h.T
    k_scr[parity] = _head_rms(_dot(h, wk_ref[...]), kg_ref[...], bd_ref[...]).astype(BF16)
    u = _dot(h, wu_ref[...])
    qt = _dot(wqt_ref[...], ht)
    for head in range(N_HEADS):
        rows = slice(head * HEAD_DIM, (head + 1) * HEAD_DIM)
        qh = qt[rows, :]
        ms = jnp.mean(qh * qh, axis=0, keepdims=True)
        qt_scr[rows, :] = (qh * lax.rsqrt(ms + EPS) * qg_ref[rows, :]).astype(BF16)
    vt_scr[parity] = _dot(wvt_ref[...], ht).astype(BF16)
    halo = jnp.where(jnp.full((POOL_HALO, D_POOL), j, jnp.int32) == 0, 0.0, halo_scr[1 - parity])
    halo_scr[parity] = u[TM_MIX - POOL_HALO:, :]

    feat = lax.broadcasted_iota(jnp.int32, (2 * HEAD_DIM, Q_BLOCK), 0)
    pen = jnp.where(jnp.full((1, LANES), j, jnp.int32) == 0, NEG_INF, 0.0)

    def pair_scores(blk, pair):
        r0 = blk * Q_BLOCK
        n_prev = TM_MIX - r0
        ls = slice(pair * LANES, (pair + 1) * LANES)
        qt2 = qt_scr[ls, r0:r0 + Q_BLOCK]
        zero = jnp.zeros_like(qt2)
        even = jnp.where(feat < HEAD_DIM, qt2, zero)
        odd = jnp.where(feat >= HEAD_DIM, qt2, zero)
        out = []
        for half in range(Q_BLOCK // LANES):
            cl = slice(half * LANES, (half + 1) * LANES)
            qmt = jnp.concatenate([even[:, cl], odd[:, cl]], axis=1)
            lo = half * KEY_GROUP
            hi = lo + BAND_GROUPS * KEY_GROUP
            out.append((_dot(k_scr[1 - parity, r0 + lo:, ls], qmt),
                        _dot(k_scr[parity, 0:hi - n_prev, ls], qmt)))
        return out

    units = [(blk, pair) for blk in range(TM_MIX // Q_BLOCK) for pair in range(N_HEADS // 2)]
    ahead = pair_scores(*units[0])
    for n, (blk, pair) in enumerate(units):
        st = ahead
        if n + 1 < len(units):
            ahead = pair_scores(*units[n + 1])
        r0 = blk * Q_BLOCK
        n_prev = TM_MIX - r0
        qcols = slice(r0, r0 + Q_BLOCK)
        for hh in range(2):
            head = 2 * pair + hh
            p_buf = pt_scr.at[2 * n + hh]
            sums = []
            for half in range(Q_BLOCK // LANES):
                cl = slice(half * LANES, (half + 1) * LANES)
                sl = slice(hh * LANES, (hh + 1) * LANES)
                st_prev, st_cur = st[half]
                lo = half * KEY_GROUP

                def scores(grp):
                    w0 = grp * KEY_GROUP
                    if w0 < n_prev:
                        blk_s = st_prev[w0 - lo:w0 - lo + KEY_GROUP, sl]
                    else:
                        blk_s = st_cur[w0 - n_prev:w0 - n_prev + KEY_GROUP, sl]
                    if _TABLE_NEEDED[grp][half]:
                        blk_s = blk_s + tblt_ref[head, w0:w0 + KEY_GROUP, cl]
                    return blk_s

                def group_pen(grp):
                    return pen if grp * KEY_GROUP < n_prev else 0.0

                groups = range(half, half + BAND_GROUPS)
                m = None
                for in_prev in (True, False):
                    part = [scores(g) for g in groups if (g * KEY_GROUP < n_prev) == in_prev]
                    if part:
                        top = jnp.max(functools.reduce(jnp.maximum, part), axis=0, keepdims=True)
                        top = top + pen if in_prev else top
                        m = top if m is None else jnp.maximum(m, top)
                acc = None
                for grp in groups:
                    e = jnp.exp2(scores(grp) - (m - group_pen(grp)))
                    acc = e if acc is None else acc + e
                    p_buf[grp * KEY_GROUP:(grp + 1) * KEY_GROUP, cl] = e.astype(BF16)
                sums.append(jnp.sum(acc, axis=0, keepdims=True))
            hrows = slice(head * HEAD_DIM, (head + 1) * HEAD_DIM)
            ot = (_dot(vt_scr[1 - parity, hrows, r0:], p_buf[0:n_prev, :])
                  + _dot(vt_scr[parity, hrows, 0:WINDOW - n_prev], p_buf[n_prev:, :]))
            catt_scr[hrows, qcols] = (ot / jnp.concatenate(sums, axis=1)).astype(BF16)

    pos = lax.broadcasted_iota(jnp.int32, (TM_MIX, 1), 0) + j * TM_MIX
    for g, w in enumerate(POOL_WINDOWS):
        ls = slice(g * POOL_GROUP_DIM, (g + 1) * POOL_GROUP_DIM)
        e = jnp.concatenate([halo[:, ls], u[:, ls]], axis=0)
        acc = e
        sh = 1
        while sh < w:
            acc = acc + pltpu.roll(acc, sh, axis=0)
            sh *= 2
        cnt = jnp.minimum(pos + 1, w).astype(F32)
        d = acc[POOL_HALO:, :] / cnt - u[:, ls]
        d_scr[:, ls] = d.astype(BF16)
    pooled = []
    for pr in range(D_POOL // MXU_DIM):
        sl = slice(pr * MXU_DIM, (pr + 1) * MXU_DIM)
        pooled.append((_dot(d_scr[:, sl], pw_ref[pr]) * ps_ref[:, sl]).astype(BF16))

    mixed = jnp.concatenate([catt_scr[...].T] + pooled, axis=1)
    o_ref[...] = x + _dot(mixed, jnp.concatenate([wouta_ref[...], woutp_ref[...]], axis=0))


def _mixer(x2d, norm_g, w_in_b, w_in_t, qg, kg, bd, tblt, pw_bd, ps, w_out_b):
    n = x2d.shape[0]

    def part(shape, *index):
        return pl.BlockSpec(shape, lambda i: index, pipeline_mode=pl.Buffered(1))

    return pl.pallas_call(
        _mixer_kernel,
        grid=(n // TM_MIX,),
        in_specs=[
            pl.BlockSpec((TM_MIX, D_MODEL), lambda i: (i, 0)),
            _resident((1, D_MODEL)),
            part((D_MODEL, D_ATTN), 0, 1),
            part((D_MODEL, D_POOL), 0, 3),
            part((D_ATTN, D_MODEL), 0, 0),
            part((D_ATTN, D_MODEL), 2, 0),
            _resident((D_ATTN, 1)),
            _resident((1, D_ATTN)),
            _resident((MXU_DIM, MXU_DIM)),
            _resident((N_HEADS, WINDOW, Q_BLOCK)),
            _resident((D_POOL // MXU_DIM, MXU_DIM, MXU_DIM)),
            _resident((1, D_POOL)),
            part((D_ATTN, D_MODEL), 0, 0),
            part((D_POOL, D_MODEL), 1, 0),
        ],
        out_specs=pl.BlockSpec((TM_MIX, D_MODEL), lambda i: (i, 0)),
        out_shape=jax.ShapeDtypeStruct((n, D_MODEL), F32),
        scratch_shapes=[
            pltpu.VMEM((D_ATTN, TM_MIX), BF16),
            pltpu.VMEM((2, TM_MIX, D_ATTN), BF16),
            pltpu.VMEM((2, D_ATTN, TM_MIX), BF16),
            pltpu.VMEM((2, POOL_HALO, D_POOL), F32),
            pltpu.VMEM((TM_MIX, D_POOL), BF16),
            pltpu.VMEM((D_ATTN, TM_MIX), BF16),
            pltpu.VMEM((UNITS, WINDOW, Q_BLOCK), BF16),
        ],
        compiler_params=pltpu.CompilerParams(
            dimension_semantics=("arbitrary",), vmem_limit_bytes=VMEM_LIMIT_BYTES),
        name="mixer",
    )(x2d, norm_g, w_in_b, w_in_b, w_in_t, w_in_t, qg, kg, bd, tblt, pw_bd, ps, w_out_b, w_out_b)


def _band_masks():
    w = np.arange(WINDOW)[:, None]
    q = np.arange(Q_BLOCK)[None, :]
    band = w - (q // CHUNK) * CHUNK
    in_band = (band >= 0) & (band < SPAN + CHUNK)
    far = in_band & (w - q <= SPAN - REL_CLIP)
    return in_band, far


def _table_needed():
    in_band, far = _band_masks()
    needed = []
    for grp in range(WINDOW // KEY_GROUP):
        rows = slice(grp * KEY_GROUP, (grp + 1) * KEY_GROUP)
        row = []
        for half in range(Q_BLOCK // LANES):
            cols = slice(half * LANES, (half + 1) * LANES)
            inside = half <= grp < half + BAND_GROUPS
            assert inside or not in_band[rows, cols].any()
            row.append(bool(inside and not far[rows, cols].all()))
        needed.append(row)
    return needed


_TABLE_NEEDED = _table_needed()


def _bias_table_t(rel_bias):
    period = 1024
    band_len = SPAN + CHUNK
    n_far = SPAN - REL_CLIP + 1
    n_near = REL_CLIP + CHUNK - 1
    rel = (rel_bias - rel_bias[:, 2 * REL_CLIP:]).astype(F32) * LOG2E
    base = jnp.concatenate([
        jnp.zeros((N_HEADS, n_far), F32),
        rel[:, 2 * REL_CLIP - 1:2 * REL_CLIP - 1 - n_near:-1],
        jnp.zeros((N_HEADS, period - n_far - n_near), F32),
    ], axis=1)
    flat = jnp.tile(base, (1, CHUNK))[:, :CHUNK * (period - 1)]
    chunk_t = flat.reshape(N_HEADS, CHUNK, period - 1)[:, :, :band_len].transpose(0, 2, 1)
    cols = [jnp.pad(chunk_t, ((0, 0), (cq * CHUNK, WINDOW - band_len - cq * CHUNK), (0, 0)),
                    constant_values=NEG_INF) for cq in range(Q_BLOCK // CHUNK)]
    return jnp.concatenate(cols, axis=2)


def _block_diag2(a, b):
    z = jnp.zeros_like(a)
    return jnp.concatenate([jnp.concatenate([a, z], axis=1),
                            jnp.concatenate([z, b], axis=1)], axis=0)


def kernel(x, ffn1_norm, ffn1_w_gate, ffn1_w_up, ffn1_w_down, mix_norm, w_in, q_norm, k_norm,
           rel_bias, pool_w, pool_scale, w_out, ffn2_norm, ffn2_w_gate, ffn2_w_up, ffn2_w_down,
           final_norm):
    b, s, d = x.shape
    assert (s, d) == (SEQ, D_MODEL) and ffn1_norm.shape[0] == 1
    x2d = x.reshape(b * s, d)
    l = 0
    head_id = jnp.arange(MXU_DIM) // HEAD_DIM
    bd = jnp.where(head_id[:, None] == head_id[None, :], 1.0 / HEAD_DIM, 0.0).astype(BF16)
    qg = (jnp.tile(q_norm[l], N_HEADS) * (HEAD_DIM ** -0.5 * LOG2E))[:, None]
    kg = jnp.tile(k_norm[l], N_HEADS)[None, :]
    pw = pool_w[l].astype(BF16)
    pw_bd = jnp.stack([_block_diag2(pw[0], pw[1]), _block_diag2(pw[2], pw[3])])
    x2d, (w_in_b, w_out_b, wg2, wu2, wd2, w_in_t) = _ffn(
        x2d, ffn1_norm[l][None, :], ffn1_w_gate[l].astype(BF16), ffn1_w_up[l].astype(BF16),
        ffn1_w_down[l].astype(BF16),
        side=(w_in[l], w_out[l], ffn2_w_gate[l], ffn2_w_up[l], ffn2_w_down[l]), side_t=(w_in[l],))
    x2d = _mixer(x2d, mix_norm[l][None, :], w_in_b, w_in_t, qg, kg, bd, _bias_table_t(rel_bias[l]),
                 pw_bd, pool_scale[l][None, :], w_out_b)
    x2d, _ = _ffn(x2d, ffn2_norm[l][None, :], wg2, wu2, wd2, final_g=final_norm[l][None, :])
    return x2d.reshape(b, s, d)
```

```python
import functools
import math

import numpy as np
import jax
import jax.numpy as jnp
from jax import lax
from jax.experimental import pallas as pl
from jax.experimental.pallas import tpu as pltpu

D_MODEL = 1024
SEQ = 4096
CHUNK = 64
N_LEFT_CHUNKS = 8
D_ATTN = 512
HEAD_DIM = 64
N_HEADS = 8
D_POOL = 512
POOL_WINDOWS = (2, 4, 8, 16)
POOL_GROUP_DIM = 128
REL_CLIP = 128
D_FF = 2816
D_IN = 3 * D_ATTN + D_POOL
EPS = 1e-6
NEG_INF = -1e30

LANES = 128
BF16_SUBLANES = 16
MXU_DIM = 256
VMEM_LIMIT_BYTES = 56 * 1024 * 1024

TM_FFN = 1024
TM_MIX = 512
TILES_PER_SEQ = SEQ // TM_MIX
Q_BLOCK = 256
SPAN = N_LEFT_CHUNKS * CHUNK
WINDOW = Q_BLOCK + SPAN
KEY_GROUP = 64
BAND_GROUPS = (SPAN + 2 * CHUNK) // KEY_GROUP
UNITS = N_HEADS * TM_MIX // Q_BLOCK
POOL_HALO = 16
LOG2E = math.log2(math.e)

BF16 = jnp.bfloat16
F32 = jnp.float32


def _rms(x, gain):
    return x * lax.rsqrt(jnp.mean(x * x, axis=-1, keepdims=True) + EPS) * gain


def _dot(a, b):
    return jnp.dot(a, b, preferred_element_type=F32)


def _ffn_kernel(*refs, final_norm, n_side, n_side_t):
    n_in = 5 + int(final_norm)
    x_ref, g_ref, wg_ref, wu_ref, wd_ref = refs[:5]
    side_in = refs[n_in:n_in + n_side]
    o_ref = refs[n_in + n_side]
    side_out = refs[n_in + n_side + 1:]
    x = x_ref[...]
    h = _rms(x, g_ref[...]).astype(BF16)
    gate = _dot(h, wg_ref[...])
    up = _dot(h, wu_ref[...])
    act = (gate * jax.nn.sigmoid(gate) * up).astype(BF16)
    y = x + 0.5 * _dot(act, wd_ref[...])
    if final_norm:
        y = _rms(y, refs[5][...])
    o_ref[...] = y
    for k, (src, dst) in enumerate(zip(side_in, side_out)):
        w = src[...]
        dst[...] = (w.T if k >= n_side - n_side_t else w).astype(BF16)


def _resident(shape):
    return pl.BlockSpec(shape, lambda i: (0,) * len(shape), pipeline_mode=pl.Buffered(1))


def _row_blocks(rows, steps):
    return max(nb for nb in range(1, steps + 1)
               if rows % nb == 0 and (rows // nb) % BF16_SUBLANES == 0)


def _ffn(x2d, norm_g, wg, wu, wd, final_g=None, side=(), side_t=()):
    n = x2d.shape[0]
    steps = n // TM_FFN
    final_norm = final_g is not None
    in_side, out_side, out_shapes = [], [], []
    for w in side:
        nb = _row_blocks(w.shape[0], steps)
        spec = pl.BlockSpec(
            (w.shape[0] // nb, w.shape[1]), lambda i, nb=nb: (jnp.minimum(i, nb - 1), 0))
        in_side.append(spec)
        out_side.append(spec)
        out_shapes.append(jax.ShapeDtypeStruct(w.shape, BF16))
    for w in side_t:
        nb = w.shape[0] // LANES
        in_side.append(pl.BlockSpec(
            (LANES, w.shape[1]), lambda i, nb=nb: (jnp.minimum(i, nb - 1), 0)))
        out_side.append(pl.BlockSpec(
            (w.shape[1], LANES), lambda i, nb=nb: (0, jnp.minimum(i, nb - 1))))
        out_shapes.append(jax.ShapeDtypeStruct(w.shape[::-1], BF16))
    tile = pl.BlockSpec((TM_FFN, D_MODEL), lambda i: (i, 0))
    outs = pl.pallas_call(
        functools.partial(_ffn_kernel, final_norm=final_norm, n_side=len(in_side),
                          n_side_t=len(side_t)),
        grid=(steps,),
        in_specs=[tile, _resident((1, D_MODEL)), _resident((D_MODEL, D_FF)),
                  _resident((D_MODEL, D_FF)), _resident((D_FF, D_MODEL))]
                 + [_resident((1, D_MODEL))] * final_norm + in_side,
        out_specs=[tile] + out_side,
        out_shape=[jax.ShapeDtypeStruct((n, D_MODEL), F32)] + out_shapes,
        compiler_params=pltpu.CompilerParams(
            dimension_semantics=("arbitrary",), vmem_limit_bytes=VMEM_LIMIT_BYTES),
        name="ffn_final" if final_norm else "ffn",
    )(x2d, norm_g, wg, wu, wd, *([final_g] if final_norm else []), *side, *side_t)
    return outs[0], outs[1:]


def _head_rms(t, gain, bd):
    sq = t * t
    hi = sq.astype(BF16)
    lo = (sq - hi.astype(F32)).astype(BF16)
    parts = []
    for s in range(D_ATTN // MXU_DIM):
        sl = slice(s * MXU_DIM, (s + 1) * MXU_DIM)
        parts.append(_dot(hi[:, sl], bd) + _dot(lo[:, sl], bd))
    ms = jnp.concatenate(parts, axis=1)
    return t * lax.rsqrt(ms + EPS) * gain


def _mixer_kernel(x_ref, g_ref, wk_ref, wu_ref, wqt_ref, wvt_ref, qg_ref, kg_ref, bd_ref, tblt_ref,
                  pw_ref, ps_ref, wouta_ref, woutp_ref, o_ref,
                  qt_scr, k_scr, vt_scr, halo_scr, d_scr, catt_scr, pt_scr):
    step = pl.program_id(0)
    j = step % TILES_PER_SEQ

    @pl.when(step == 0)
    def _():
        k_scr[...] = jnp.zeros(k_scr.shape, BF16)
        vt_scr[...] = jnp.zeros(vt_scr.shape, BF16)
        halo_scr[...] = jnp.zeros(halo_scr.shape, F32)
        pt_scr[...] = jnp.zeros(pt_scr.shape, BF16)

    parity = step % 2

    x = x_ref[...]
    hf = _rms(x, g_ref[...])
    h = hf.astype(BF16)
    ht = h.T
    k_scr[parity] = _head_rms(_dot(h, wk_ref[...]), kg_ref[...], bd_ref[...]).astype(BF16)
    u = _dot(h, wu_ref[...])
    qt = _dot(wqt_ref[...], ht)
    for head in range(N_HEADS):
        rows = slice(head * HEAD_DIM, (head + 1) * HEAD_DIM)
        qh = qt[rows, :]
        ms = jnp.mean(qh * qh, axis=0, keepdims=True)
        qt_scr[rows, :] = (qh * lax.rsqrt(ms + EPS) * qg_ref[rows, :]).astype(BF16)
    vt_scr[parity] = _dot(wvt_ref[...], ht).astype(BF16)
    halo = jnp.where(jnp.full((POOL_HALO, D_POOL), j, jnp.int32) == 0, 0.0, halo_scr[1 - parity])
    halo_scr[parity] = u[TM_MIX - POOL_HALO:, :]

    feat = lax.broadcasted_iota(jnp.int32, (2 * HEAD_DIM, Q_BLOCK), 0)
    pen = jnp.where(jnp.full((1, LANES), j, jnp.int32) == 0, NEG_INF, 0.0)

    def pair_scores(blk, pair):
        r0 = blk * Q_BLOCK
        n_prev = TM_MIX - r0
        ls = slice(pair * LANES, (pair + 1) * LANES)
        qt2 = qt_scr[ls, r0:r0 + Q_BLOCK]
        zero = jnp.zeros_like(qt2)
        even = jnp.where(feat < HEAD_DIM, qt2, zero)
        odd = jnp.where(feat >= HEAD_DIM, qt2, zero)
        out = []
        for half in range(Q_BLOCK // LANES):
            cl = slice(half * LANES, (half + 1) * LANES)
            qmt = jnp.concatenate([even[:, cl], odd[:, cl]], axis=1)
            lo = half * LANES
            hi = lo + BAND_GROUPS * KEY_GROUP
            out.append((_dot(k_scr[1 - parity, r0 + lo:, ls], qmt),
                        _dot(k_scr[parity, 0:hi - n_prev, ls], qmt)))
        return out

    units = [(blk, pair) for blk in range(TM_MIX // Q_BLOCK) for pair in range(N_HEADS // 2)]
    ahead = pair_scores(*units[0])
    for n, (blk, pair) in enumerate(units):
        st = ahead
        if n + 1 < len(units):
            ahead = pair_scores(*units[n + 1])
        r0 = blk * Q_BLOCK
        n_prev = TM_MIX - r0
        qcols = slice(r0, r0 + Q_BLOCK)
        for hh in range(2):
            head = 2 * pair + hh
            p_buf = pt_scr.at[2 * n + hh]
            sums = []
            for half in range(Q_BLOCK // LANES):
                cl = slice(half * LANES, (half + 1) * LANES)
                sl = slice(hh * LANES, (hh + 1) * LANES)
                st_prev, st_cur = st[half]
                lo = half * LANES

                def scores(grp):
                    w0 = grp * KEY_GROUP
                    if w0 < n_prev:
                        blk_s = st_prev[w0 - lo:w0 - lo + KEY_GROUP, sl]
                    else:
                        blk_s = st_cur[w0 - n_prev:w0 - n_prev + KEY_GROUP, sl]
                    if _TABLE_NEEDED[grp][half]:
                        blk_s = blk_s + tblt_ref[head, w0:w0 + KEY_GROUP, cl]
                    return blk_s

                def group_pen(grp):
                    return pen if grp * KEY_GROUP < n_prev else 0.0

                first_grp = half * LANES // KEY_GROUP
                groups = range(first_grp, first_grp + BAND_GROUPS)
                m = None
                for in_prev in (True, False):
                    part = [scores(g) for g in groups if (g * KEY_GROUP < n_prev) == in_prev]
                    if part:
                        top = jnp.max(functools.reduce(jnp.maximum, part), axis=0, keepdims=True)
                        top = top + pen if in_prev else top
                        m = top if m is None else jnp.maximum(m, top)
                acc = None
                for grp in groups:
                    e = jnp.exp2(scores(grp) - (m - group_pen(grp)))
                    acc = e if acc is None else acc + e
                    p_buf[grp * KEY_GROUP:(grp + 1) * KEY_GROUP, cl] = e.astype(BF16)
                sums.append(jnp.sum(acc, axis=0, keepdims=True))
            hrows = slice(head * HEAD_DIM, (head + 1) * HEAD_DIM)
            ot = (_dot(vt_scr[1 - parity, hrows, r0:], p_buf[0:n_prev, :])
                  + _dot(vt_scr[parity, hrows, 0:WINDOW - n_prev], p_buf[n_prev:, :]))
            catt_scr[hrows, qcols] = (ot / jnp.concatenate(sums, axis=1)).astype(BF16)

    pos = lax.broadcasted_iota(jnp.int32, (TM_MIX, 1), 0) + j * TM_MIX
    for g, w in enumerate(POOL_WINDOWS):
        ls = slice(g * POOL_GROUP_DIM, (g + 1) * POOL_GROUP_DIM)
        e = jnp.concatenate([halo[:, ls], u[:, ls]], axis=0)
        acc = e
        sh = 1
        while sh < w:
            acc = acc + pltpu.roll(acc, sh, axis=0)
            sh *= 2
        cnt = jnp.minimum(pos + 1, w).astype(F32)
        d = acc[POOL_HALO:, :] / cnt - u[:, ls]
        d_scr[:, ls] = d.astype(BF16)
    pooled = []
    for pr in range(D_POOL // MXU_DIM):
        sl = slice(pr * MXU_DIM, (pr + 1) * MXU_DIM)
        pooled.append((_dot(d_scr[:, sl], pw_ref[pr]) * ps_ref[:, sl]).astype(BF16))

    mixed = jnp.concatenate([catt_scr[...].T] + pooled, axis=1)
    o_ref[...] = x + _dot(mixed, jnp.concatenate([wouta_ref[...], woutp_ref[...]], axis=0))


def _mixer(x2d, norm_g, w_in_b, w_in_t, qg, kg, bd, tblt, pw_bd, ps, w_out_b):
    n = x2d.shape[0]

    def part(shape, *index):
        return pl.BlockSpec(shape, lambda i: index, pipeline_mode=pl.Buffered(1))

    return pl.pallas_call(
        _mixer_kernel,
        grid=(n // TM_MIX,),
        in_specs=[
            pl.BlockSpec((TM_MIX, D_MODEL), lambda i: (i, 0)),
            _resident((1, D_MODEL)),
            part((D_MODEL, D_ATTN), 0, 1),
            part((D_MODEL, D_POOL), 0, 3),
            part((D_ATTN, D_MODEL), 0, 0),
            part((D_ATTN, D_MODEL), 2, 0),
            _resident((D_ATTN, 1)),
            _resident((1, D_ATTN)),
            _resident((MXU_DIM, MXU_DIM)),
            _resident((N_HEADS, WINDOW, Q_BLOCK)),
            _resident((D_POOL // MXU_DIM, MXU_DIM, MXU_DIM)),
            _resident((1, D_POOL)),
            part((D_ATTN, D_MODEL), 0, 0),
            part((D_POOL, D_MODEL), 1, 0),
        ],
        out_specs=pl.BlockSpec((TM_MIX, D_MODEL), lambda i: (i, 0)),
        out_shape=jax.ShapeDtypeStruct((n, D_MODEL), F32),
        scratch_shapes=[
            pltpu.VMEM((D_ATTN, TM_MIX), BF16),
            pltpu.VMEM((2, TM_MIX, D_ATTN), BF16),
            pltpu.VMEM((2, D_ATTN, TM_MIX), BF16),
            pltpu.VMEM((2, POOL_HALO, D_POOL), F32),
            pltpu.VMEM((TM_MIX, D_POOL), BF16),
            pltpu.VMEM((D_ATTN, TM_MIX), BF16),
            pltpu.VMEM((UNITS, WINDOW, Q_BLOCK), BF16),
        ],
        compiler_params=pltpu.CompilerParams(
            dimension_semantics=("arbitrary",), vmem_limit_bytes=VMEM_LIMIT_BYTES),
        name="mixer",
    )(x2d, norm_g, w_in_b, w_in_b, w_in_t, w_in_t, qg, kg, bd, tblt, pw_bd, ps, w_out_b, w_out_b)


def _band_masks():
    w = np.arange(WINDOW)[:, None]
    q = np.arange(Q_BLOCK)[None, :]
    band = w - (q // CHUNK) * CHUNK
    in_band = (band >= 0) & (band < SPAN + CHUNK)
    far = in_band & (w - q <= SPAN - REL_CLIP)
    return in_band, far


def _table_needed():
    in_band, far = _band_masks()
    needed = []
    for grp in range(WINDOW // KEY_GROUP):
        rows = slice(grp * KEY_GROUP, (grp + 1) * KEY_GROUP)
        row = []
        for half in range(Q_BLOCK // LANES):
            cols = slice(half * LANES, (half + 1) * LANES)
            inside = half * LANES <= grp * KEY_GROUP < half * LANES + BAND_GROUPS * KEY_GROUP
            assert inside or not in_band[rows, cols].any()
            row.append(bool(inside and not far[rows, cols].all()))
        needed.append(row)
    return needed


_TABLE_NEEDED = _table_needed()


def _bias_table_t(rel_bias):
    period = 1024
    band_len = SPAN + CHUNK
    n_far = SPAN - REL_CLIP + 1
    n_near = REL_CLIP + CHUNK - 1
    rel = (rel_bias - rel_bias[:, 2 * REL_CLIP:]).astype(F32) * LOG2E
    base = jnp.concatenate([
        jnp.zeros((N_HEADS, n_far), F32),
        rel[:, 2 * REL_CLIP - 1:2 * REL_CLIP - 1 - n_near:-1],
        jnp.zeros((N_HEADS, period - n_far - n_near), F32),
    ], axis=1)
    flat = jnp.tile(base, (1, CHUNK))[:, :CHUNK * (period - 1)]
    chunk_t = flat.reshape(N_HEADS, CHUNK, period - 1)[:, :, :band_len].transpose(0, 2, 1)
    cols = [jnp.pad(chunk_t, ((0, 0), (cq * CHUNK, WINDOW - band_len - cq * CHUNK), (0, 0)),
                    constant_values=NEG_INF) for cq in range(Q_BLOCK // CHUNK)]
    return jnp.concatenate(cols, axis=2)


def _block_diag2(a, b):
    z = jnp.zeros_like(a)
    return jnp.concatenate([jnp.concatenate([a, z], axis=1),
                            jnp.concatenate([z, b], axis=1)], axis=0)


def kernel(x, ffn1_norm, ffn1_w_gate, ffn1_w_up, ffn1_w_down, mix_norm, w_in, q_norm, k_norm,
           rel_bias, pool_w, pool_scale, w_out, ffn2_norm, ffn2_w_gate, ffn2_w_up, ffn2_w_down,
           final_norm):
    b, s, d = x.shape
    assert (s, d) == (SEQ, D_MODEL) and ffn1_norm.shape[0] == 1
    x2d = x.reshape(b * s, d)
    l = 0
    head_id = jnp.arange(MXU_DIM) // HEAD_DIM
    bd = jnp.where(head_id[:, None] == head_id[None, :], 1.0 / HEAD_DIM, 0.0).astype(BF16)
    qg = (jnp.tile(q_norm[l], N_HEADS) * (HEAD_DIM ** -0.5 * LOG2E))[:, None]
    kg = jnp.tile(k_norm[l], N_HEADS)[None, :]
    pw = pool_w[l].astype(BF16)
    pw_bd = jnp.stack([_block_diag2(pw[0], pw[1]), _block_diag2(pw[2], pw[3])])
    x2d, (w_in_b, w_out_b, wg2, wu2, wd2, w_in_t) = _ffn(
        x2d, ffn1_norm[l][None, :], ffn1_w_gate[l].astype(BF16), ffn1_w_up[l].astype(BF16),
        ffn1_w_down[l].astype(BF16),
        side=(w_in[l], w_out[l], ffn2_w_gate[l], ffn2_w_up[l], ffn2_w_down[l]), side_t=(w_in[l],))
    x2d = _mixer(x2d, mix_norm[l][None, :], w_in_b, w_in_t, qg, kg, bd, _bias_table_t(rel_bias[l]),
                 pw_bd, pool_scale[l][None, :], w_out_b)
    x2d, _ = _ffn(x2d, ffn2_norm[l][None, :], wg2, wu2, wd2, final_g=final_norm[l][None, :])
    return x2d.reshape(b, s, d)
```

```python
import functools
import math

import numpy as np
import jax
import jax.numpy as jnp
from jax import lax
from jax.experimental import pallas as pl
from jax.experimental.pallas import tpu as pltpu

D_MODEL = 1024
SEQ = 4096
CHUNK = 64
N_LEFT_CHUNKS = 8
D_ATTN = 512
HEAD_DIM = 64
N_HEADS = 8
D_POOL = 512
POOL_WINDOWS = (2, 4, 8, 16)
POOL_GROUP_DIM = 128
REL_CLIP = 128
D_FF = 2816
D_IN = 3 * D_ATTN + D_POOL
EPS = 1e-6
NEG_INF = -1e30

LANES = 128
BF16_SUBLANES = 16
MXU_DIM = 256
VMEM_LIMIT_BYTES = 56 * 1024 * 1024

TM_FFN = 1024
STAGE_ROWS_UP = 32
STAGE_ROWS_DOWN = 176
TM_MIX = 512
TILES_PER_SEQ = SEQ // TM_MIX
Q_BLOCK = 256
SPAN = N_LEFT_CHUNKS * CHUNK
WINDOW = Q_BLOCK + SPAN
KEY_GROUP = 64
BAND_GROUPS = (SPAN + 2 * CHUNK) // KEY_GROUP
UNITS = N_HEADS * TM_MIX // Q_BLOCK
POOL_HALO = 16
LOG2E = math.log2(math.e)

BF16 = jnp.bfloat16
F32 = jnp.float32


def _rms(x, gain):
    return x * lax.rsqrt(jnp.mean(x * x, axis=-1, keepdims=True) + EPS) * gain


def _dot(a, b):
    return jnp.dot(a, b, preferred_element_type=F32)


def _cast_side(side_in, side_out, n_transposed):
    for k, (src, dst) in enumerate(zip(side_in, side_out)):
        w = src[...]
        dst[...] = (w.T if k >= len(side_in) - n_transposed else w).astype(BF16)


def _side_specs(side, side_t, steps):
    in_specs, out_specs, out_shapes = [], [], []
    for w in side:
        nb = max(nb for nb in range(1, steps + 1)
                 if w.shape[0] % nb == 0 and (w.shape[0] // nb) % BF16_SUBLANES == 0)
        spec = pl.BlockSpec(
            (w.shape[0] // nb, w.shape[1]), lambda i, nb=nb: (jnp.minimum(i, nb - 1), 0))
        in_specs.append(spec)
        out_specs.append(spec)
        out_shapes.append(jax.ShapeDtypeStruct(w.shape, BF16))
    for w in side_t:
        nb = w.shape[0] // LANES
        in_specs.append(pl.BlockSpec(
            (LANES, w.shape[1]), lambda i, nb=nb: (jnp.minimum(i, nb - 1), 0)))
        out_specs.append(pl.BlockSpec(
            (w.shape[1], LANES), lambda i, nb=nb: (0, jnp.minimum(i, nb - 1))))
        out_shapes.append(jax.ShapeDtypeStruct(w.shape[::-1], BF16))
    return in_specs, out_specs, out_shapes


def _load_cast(src_hbm, dst, stage, sem, rows):
    n = src_hbm.shape[0] // rows

    def copy(c):
        return pltpu.make_async_copy(src_hbm.at[pl.ds(c * rows, rows), :], stage.at[c % 2],
                                     sem.at[c % 2])

    copy(0).start()
    for c in range(n):
        if c + 1 < n:
            copy(c + 1).start()
        copy(c).wait()
        dst[c * rows:(c + 1) * rows, :] = stage[c % 2].astype(BF16)


def _ffn_kernel(*refs, final_norm, n_side, n_side_t, own_weights):
    n_in = 5 + int(final_norm)
    x_ref, g_ref, wg_ref, wu_ref, wd_ref = refs[:5]
    side_in = refs[n_in:n_in + n_side]
    o_ref = refs[n_in + n_side]
    side_out = refs[n_in + n_side + 1:n_in + 2 * n_side + 1]
    if own_weights:
        wg_b, wu_b, wd_b, stage_up, stage_down, sem = refs[n_in + 2 * n_side + 1:]

        @pl.when(pl.program_id(0) == 0)
        def _():
            _load_cast(wg_ref, wg_b, stage_up, sem, STAGE_ROWS_UP)
            _load_cast(wu_ref, wu_b, stage_up, sem, STAGE_ROWS_UP)
            _load_cast(wd_ref, wd_b, stage_down, sem, STAGE_ROWS_DOWN)

        wg_ref, wu_ref, wd_ref = wg_b, wu_b, wd_b
    x = x_ref[...]
    h = _rms(x, g_ref[...]).astype(BF16)
    gate = _dot(h, wg_ref[...])
    up = _dot(h, wu_ref[...])
    act = (gate * jax.nn.sigmoid(gate) * up).astype(BF16)
    y = x + 0.5 * _dot(act, wd_ref[...])
    if final_norm:
        y = _rms(y, refs[5][...])
    o_ref[...] = y
    _cast_side(side_in, side_out, n_side_t)


def _resident(shape):
    return pl.BlockSpec(shape, lambda i: (0,) * len(shape), pipeline_mode=pl.Buffered(1))


def _ffn(x2d, norm_g, wg, wu, wd, final_g=None, side=(), side_t=()):
    n = x2d.shape[0]
    steps = n // TM_FFN
    final_norm = final_g is not None
    own_weights = wg.dtype == F32
    assert wu.dtype == wg.dtype == wd.dtype
    in_side, out_side, out_shapes = _side_specs(side, side_t, steps)
    tile = pl.BlockSpec((TM_FFN, D_MODEL), lambda i: (i, 0))
    if own_weights:
        weight_specs = [pl.BlockSpec(memory_space=pl.ANY)] * 3
        scratch = [pltpu.VMEM((D_MODEL, D_FF), BF16), pltpu.VMEM((D_MODEL, D_FF), BF16),
                   pltpu.VMEM((D_FF, D_MODEL), BF16),
                   pltpu.VMEM((2, STAGE_ROWS_UP, D_FF), F32),
                   pltpu.VMEM((2, STAGE_ROWS_DOWN, D_MODEL), F32),
                   pltpu.SemaphoreType.DMA((2,))]
    else:
        weight_specs = [_resident((D_MODEL, D_FF)), _resident((D_MODEL, D_FF)),
                        _resident((D_FF, D_MODEL))]
        scratch = []
    outs = pl.pallas_call(
        functools.partial(_ffn_kernel, final_norm=final_norm, n_side=len(in_side),
                          n_side_t=len(side_t), own_weights=own_weights),
        grid=(steps,),
        in_specs=[tile, _resident((1, D_MODEL))] + weight_specs
                 + [_resident((1, D_MODEL))] * final_norm + in_side,
        out_specs=[tile] + out_side,
        out_shape=[jax.ShapeDtypeStruct((n, D_MODEL), F32)] + out_shapes,
        scratch_shapes=scratch,
        compiler_params=pltpu.CompilerParams(
            dimension_semantics=("arbitrary",), vmem_limit_bytes=VMEM_LIMIT_BYTES),
        name="ffn_final" if final_norm else "ffn",
    )(x2d, norm_g, wg, wu, wd, *([final_g] if final_norm else []), *side, *side_t)
    return outs[0], outs[1:]


def _head_rms(t, gain, bd):
    sq = (t * t).astype(BF16)
    ms = jnp.concatenate([_dot(sq[:, s * MXU_DIM:(s + 1) * MXU_DIM], bd)
                          for s in range(D_ATTN // MXU_DIM)], axis=1)
    return t * lax.rsqrt(ms + EPS) * gain


def _mixer_kernel(x_ref, g_ref, wk_ref, wu_ref, wqt_ref, wvt_ref, qg_ref, kg_ref, bd_ref, tblt_ref,
                  pw_ref, ps_ref, wouta_ref, woutp_ref, *rest):
    n_side = (len(rest) - 8) // 2
    side_in, o_ref, side_out = rest[:n_side], rest[n_side], rest[n_side + 1:2 * n_side + 1]
    qt_scr, k_scr, vt_scr, halo_scr, d_scr, catt_scr, pt_scr = rest[2 * n_side + 1:]
    step = pl.program_id(0)
    j = step % TILES_PER_SEQ

    @pl.when(step == 0)
    def _():
        k_scr[...] = jnp.zeros(k_scr.shape, BF16)
        vt_scr[...] = jnp.zeros(vt_scr.shape, BF16)
        halo_scr[...] = jnp.zeros(halo_scr.shape, F32)
        pt_scr[...] = jnp.zeros(pt_scr.shape, BF16)

    parity = step % 2

    x = x_ref[...]
    hf = _rms(x, g_ref[...])
    h = hf.astype(BF16)
    ht = h.T
    k_scr[parity] = _head_rms(_dot(h, wk_ref[...]), kg_ref[...], bd_ref[...]).astype(BF16)
    u = _dot(h, wu_ref[...])
    qt = _dot(wqt_ref[...], ht)
    for head in range(N_HEADS):
        rows = slice(head * HEAD_DIM, (head + 1) * HEAD_DIM)
        qh = qt[rows, :]
        ms = jnp.mean(qh * qh, axis=0, keepdims=True)
        qt_scr[rows, :] = (qh * lax.rsqrt(ms + EPS) * qg_ref[rows, :]).astype(BF16)
    vt_scr[parity] = _dot(wvt_ref[...], ht).astype(BF16)
    halo = jnp.where(jnp.full((POOL_HALO, D_POOL), j, jnp.int32) == 0, 0.0, halo_scr[1 - parity])
    halo_scr[parity] = u[TM_MIX - POOL_HALO:, :]

    feat = lax.broadcasted_iota(jnp.int32, (2 * HEAD_DIM, Q_BLOCK), 0)
    pen = jnp.where(jnp.full((1, LANES), j, jnp.int32) == 0, NEG_INF, 0.0)

    def pair_scores(blk, pair):
        r0 = blk * Q_BLOCK
        n_prev = TM_MIX - r0
        ls = slice(pair * LANES, (pair + 1) * LANES)
        qt2 = qt_scr[ls, r0:r0 + Q_BLOCK]
        zero = jnp.zeros_like(qt2)
        even = jnp.where(feat < HEAD_DIM, qt2, zero)
        odd = jnp.where(feat >= HEAD_DIM, qt2, zero)
        out = []
        for half in range(Q_BLOCK // LANES):
            cl = slice(half * LANES, (half + 1) * LANES)
            qmt = jnp.concatenate([even[:, cl], odd[:, cl]], axis=1)
            lo = half * LANES
            hi = lo + BAND_GROUPS * KEY_GROUP
            out.append((_dot(k_scr[1 - parity, r0 + lo:, ls], qmt),
                        _dot(k_scr[parity, 0:hi - n_prev, ls], qmt)))
        return out

    units = [(blk, pair) for blk in range(TM_MIX // Q_BLOCK) for pair in range(N_HEADS // 2)]
    ahead = pair_scores(*units[0])
    for n, (blk, pair) in enumerate(units):
        st = ahead
        if n + 1 < len(units):
            ahead = pair_scores(*units[n + 1])
        r0 = blk * Q_BLOCK
        n_prev = TM_MIX - r0
        qcols = slice(r0, r0 + Q_BLOCK)
        for hh in range(2):
            head = 2 * pair + hh
            p_buf = pt_scr.at[2 * n + hh]
            sums = []
            for half in range(Q_BLOCK // LANES):
                cl = slice(half * LANES, (half + 1) * LANES)
                sl = slice(hh * LANES, (hh + 1) * LANES)
                st_prev, st_cur = st[half]
                lo = half * LANES

                def scores(grp):
                    w0 = grp * KEY_GROUP
                    if w0 < n_prev:
                        blk_s = st_prev[w0 - lo:w0 - lo + KEY_GROUP, sl]
                    else:
                        blk_s = st_cur[w0 - n_prev:w0 - n_prev + KEY_GROUP, sl]
                    if _TABLE_NEEDED[grp][half]:
                        blk_s = blk_s + tblt_ref[head, w0:w0 + KEY_GROUP, cl]
                    return blk_s

                def group_pen(grp):
                    return pen if grp * KEY_GROUP < n_prev else 0.0

                first_grp = half * LANES // KEY_GROUP
                groups = range(first_grp, first_grp + BAND_GROUPS)
                m = None
                for in_prev in (True, False):
                    part = [scores(g) for g in groups if (g * KEY_GROUP < n_prev) == in_prev]
                    if part:
                        top = jnp.max(functools.reduce(jnp.maximum, part), axis=0, keepdims=True)
                        top = top + pen if in_prev else top
                        m = top if m is None else jnp.maximum(m, top)
                acc = None
                for grp in groups:
                    e = jnp.exp2(scores(grp) - (m - group_pen(grp)))
                    acc = e if acc is None else acc + e
                    p_buf[grp * KEY_GROUP:(grp + 1) * KEY_GROUP, cl] = e.astype(BF16)
                sums.append(jnp.sum(acc, axis=0, keepdims=True))
            hrows = slice(head * HEAD_DIM, (head + 1) * HEAD_DIM)
            ot = (_dot(vt_scr[1 - parity, hrows, r0:], p_buf[0:n_prev, :])
                  + _dot(vt_scr[parity, hrows, 0:WINDOW - n_prev], p_buf[n_prev:, :]))
            catt_scr[hrows, qcols] = (ot / jnp.concatenate(sums, axis=1)).astype(BF16)

    pos = lax.broadcasted_iota(jnp.int32, (TM_MIX, 1), 0) + j * TM_MIX
    for g, w in enumerate(POOL_WINDOWS):
        ls = slice(g * POOL_GROUP_DIM, (g + 1) * POOL_GROUP_DIM)
        e = jnp.concatenate([halo[:, ls], u[:, ls]], axis=0)
        acc = e
        sh = 1
        while sh < w:
            acc = acc + pltpu.roll(acc, sh, axis=0)
            sh *= 2
        cnt = jnp.minimum(pos + 1, w).astype(F32)
        d = acc[POOL_HALO:, :] / cnt - u[:, ls]
        d_scr[:, ls] = d.astype(BF16)
    pooled = []
    for pr in range(D_POOL // MXU_DIM):
        sl = slice(pr * MXU_DIM, (pr + 1) * MXU_DIM)
        pooled.append((_dot(d_scr[:, sl], pw_ref[pr]) * ps_ref[:, sl]).astype(BF16))

    mixed = jnp.concatenate([catt_scr[...].T] + pooled, axis=1)
    o_ref[...] = x + _dot(mixed, jnp.concatenate([wouta_ref[...], woutp_ref[...]], axis=0))
    _cast_side(side_in, side_out, 0)


def _mixer(x2d, norm_g, w_in_b, w_in_t, qg, kg, bd, tblt, pw_bd, ps, w_out_b, side=()):
    n = x2d.shape[0]
    in_side, out_side, out_shapes = _side_specs(side, (), n // TM_MIX)
    tile = pl.BlockSpec((TM_MIX, D_MODEL), lambda i: (i, 0))

    def part(shape, *index):
        return pl.BlockSpec(shape, lambda i: index, pipeline_mode=pl.Buffered(1))

    outs = pl.pallas_call(
        _mixer_kernel,
        grid=(n // TM_MIX,),
        in_specs=[
            tile,
            _resident((1, D_MODEL)),
            part((D_MODEL, D_ATTN), 0, 1),
            part((D_MODEL, D_POOL), 0, 3),
            part((D_ATTN, D_MODEL), 0, 0),
            part((D_ATTN, D_MODEL), 2, 0),
            _resident((D_ATTN, 1)),
            _resident((1, D_ATTN)),
            _resident((MXU_DIM, MXU_DIM)),
            _resident((N_HEADS, WINDOW, Q_BLOCK)),
            _resident((D_POOL // MXU_DIM, MXU_DIM, MXU_DIM)),
            _resident((1, D_POOL)),
            part((D_ATTN, D_MODEL), 0, 0),
            part((D_POOL, D_MODEL), 1, 0),
        ] + in_side,
        out_specs=[tile] + out_side,
        out_shape=[jax.ShapeDtypeStruct((n, D_MODEL), F32)] + out_shapes,
        scratch_shapes=[
            pltpu.VMEM((D_ATTN, TM_MIX), BF16),
            pltpu.VMEM((2, TM_MIX, D_ATTN), BF16),
            pltpu.VMEM((2, D_ATTN, TM_MIX), BF16),
            pltpu.VMEM((2, POOL_HALO, D_POOL), F32),
            pltpu.VMEM((TM_MIX, D_POOL), BF16),
            pltpu.VMEM((D_ATTN, TM_MIX), BF16),
            pltpu.VMEM((UNITS, WINDOW, Q_BLOCK), BF16),
        ],
        compiler_params=pltpu.CompilerParams(
            dimension_semantics=("arbitrary",), vmem_limit_bytes=VMEM_LIMIT_BYTES),
        name="mixer",
    )(x2d, norm_g, w_in_b, w_in_b, w_in_t, w_in_t, qg, kg, bd, tblt, pw_bd, ps, w_out_b, w_out_b,
      *side)
    return outs[0], outs[1:]


def _band_masks():
    w = np.arange(WINDOW)[:, None]
    q = np.arange(Q_BLOCK)[None, :]
    band = w - (q // CHUNK) * CHUNK
    in_band = (band >= 0) & (band < SPAN + CHUNK)
    far = in_band & (w - q <= SPAN - REL_CLIP)
    return in_band, far


def _table_needed():
    in_band, far = _band_masks()
    needed = []
    for grp in range(WINDOW // KEY_GROUP):
        rows = slice(grp * KEY_GROUP, (grp + 1) * KEY_GROUP)
        row = []
        for half in range(Q_BLOCK // LANES):
            cols = slice(half * LANES, (half + 1) * LANES)
            inside = half * LANES <= grp * KEY_GROUP < half * LANES + BAND_GROUPS * KEY_GROUP
            assert inside or not in_band[rows, cols].any()
            row.append(bool(inside and not far[rows, cols].all()))
        needed.append(row)
    return needed


_TABLE_NEEDED = _table_needed()


def _bias_table_t(rel_bias):
    period = 1024
    band_len = SPAN + CHUNK
    n_far = SPAN - REL_CLIP + 1
    n_near = REL_CLIP + CHUNK - 1
    rel = (rel_bias - rel_bias[:, 2 * REL_CLIP:]).astype(F32) * LOG2E
    base = jnp.concatenate([
        jnp.zeros((N_HEADS, n_far), F32),
        rel[:, 2 * REL_CLIP - 1:2 * REL_CLIP - 1 - n_near:-1],
        jnp.zeros((N_HEADS, period - n_far - n_near), F32),
    ], axis=1)
    flat = jnp.tile(base, (1, CHUNK))[:, :CHUNK * (period - 1)]
    chunk_t = flat.reshape(N_HEADS, CHUNK, period - 1)[:, :, :band_len].transpose(0, 2, 1)
    cols = [jnp.pad(chunk_t, ((0, 0), (cq * CHUNK, WINDOW - band_len - cq * CHUNK), (0, 0)),
                    constant_values=NEG_INF) for cq in range(Q_BLOCK // CHUNK)]
    return jnp.concatenate(cols, axis=2)


def _block_diag2(a, b):
    z = jnp.zeros_like(a)
    return jnp.concatenate([jnp.concatenate([a, z], axis=1),
                            jnp.concatenate([z, b], axis=1)], axis=0)


def kernel(x, ffn1_norm, ffn1_w_gate, ffn1_w_up, ffn1_w_down, mix_norm, w_in, q_norm, k_norm,
           rel_bias, pool_w, pool_scale, w_out, ffn2_norm, ffn2_w_gate, ffn2_w_up, ffn2_w_down,
           final_norm):
    b, s, d = x.shape
    assert (s, d) == (SEQ, D_MODEL) and ffn1_norm.shape[0] == 1
    x2d = x.reshape(b * s, d)
    l = 0
    head_id = jnp.arange(MXU_DIM) // HEAD_DIM
    bd = jnp.where(head_id[:, None] == head_id[None, :], 1.0 / HEAD_DIM, 0.0).astype(BF16)
    qg = (jnp.tile(q_norm[l], N_HEADS) * (HEAD_DIM ** -0.5 * LOG2E))[:, None]
    kg = jnp.tile(k_norm[l], N_HEADS)[None, :]
    pw = pool_w[l].astype(BF16)
    pw_bd = jnp.stack([_block_diag2(pw[0], pw[1]), _block_diag2(pw[2], pw[3])])
    x2d, (w_in_b, w_out_b, w_in_t) = _ffn(
        x2d, ffn1_norm[l][None, :], ffn1_w_gate[l], ffn1_w_up[l], ffn1_w_down[l],
        side=(w_in[l], w_out[l]), side_t=(w_in[l],))
    x2d, (wg2, wu2, wd2) = _mixer(
        x2d, mix_norm[l][None, :], w_in_b, w_in_t, qg, kg, bd, _bias_table_t(rel_bias[l]),
        pw_bd, pool_scale[l][None, :], w_out_b,
        side=(ffn2_w_gate[l], ffn2_w_up[l], ffn2_w_down[l]))
    x2d, _ = _ffn(x2d, ffn2_norm[l][None, :], wg2, wu2, wd2, final_g=final_norm[l][None, :])
    return x2d.reshape(b, s, d)
```

```python
import functools
import math

import numpy as np
import jax
import jax.numpy as jnp
from jax import lax
from jax.experimental import pallas as pl
from jax.experimental.pallas import tpu as pltpu

D_MODEL = 1024
SEQ = 4096
CHUNK = 64
N_LEFT_CHUNKS = 8
D_ATTN = 512
HEAD_DIM = 64
N_HEADS = 8
D_POOL = 512
POOL_WINDOWS = (2, 4, 8, 16)
POOL_GROUP_DIM = 128
REL_CLIP = 128
D_FF = 2816
D_IN = 3 * D_ATTN + D_POOL
EPS = 1e-6
NEG_INF = -1e30

LANES = 128
BF16_SUBLANES = 16
MXU_DIM = 256
VMEM_LIMIT_BYTES = 56 * 1024 * 1024

TM_FFN = 1024
TM_MIX = 512
TILES_PER_SEQ = SEQ // TM_MIX
Q_BLOCK = 256
SPAN = N_LEFT_CHUNKS * CHUNK
WINDOW = Q_BLOCK + SPAN
KEY_GROUP = 64
BAND_GROUPS = (SPAN + 2 * CHUNK) // KEY_GROUP
POOL_HALO = 16
LOG2E = math.log2(math.e)

BF16 = jnp.bfloat16
F32 = jnp.float32


def _rms(x, gain):
    return x * lax.rsqrt(jnp.mean(x * x, axis=-1, keepdims=True) + EPS) * gain


def _dot(a, b):
    return jnp.dot(a, b, preferred_element_type=F32)


def _ffn_kernel(*refs, final_norm, n_side, n_side_t):
    n_in = 5 + int(final_norm)
    x_ref, g_ref, wg_ref, wu_ref, wd_ref = refs[:5]
    side_in = refs[n_in:n_in + n_side]
    o_ref = refs[n_in + n_side]
    side_out = refs[n_in + n_side + 1:]
    x = x_ref[...]
    h = _rms(x, g_ref[...]).astype(BF16)
    gate = _dot(h, wg_ref[...])
    up = _dot(h, wu_ref[...])
    act = (gate * jax.nn.sigmoid(gate) * up).astype(BF16)
    y = x + 0.5 * _dot(act, wd_ref[...])
    if final_norm:
        y = _rms(y, refs[5][...])
    o_ref[...] = y
    for k, (src, dst) in enumerate(zip(side_in, side_out)):
        w = src[...]
        dst[...] = (w.T if k >= n_side - n_side_t else w).astype(BF16)


def _resident(shape):
    return pl.BlockSpec(shape, lambda i: (0,) * len(shape), pipeline_mode=pl.Buffered(1))


def _row_blocks(rows, steps):
    return max(nb for nb in range(1, steps + 1)
               if rows % nb == 0 and (rows // nb) % BF16_SUBLANES == 0)


def _ffn(x2d, norm_g, wg, wu, wd, final_g=None, side=(), side_t=()):
    n = x2d.shape[0]
    steps = n // TM_FFN
    final_norm = final_g is not None
    in_side, out_side, out_shapes = [], [], []
    for w in side:
        nb = _row_blocks(w.shape[0], steps)
        spec = pl.BlockSpec(
            (w.shape[0] // nb, w.shape[1]), lambda i, nb=nb: (jnp.minimum(i, nb - 1), 0))
        in_side.append(spec)
        out_side.append(spec)
        out_shapes.append(jax.ShapeDtypeStruct(w.shape, BF16))
    for w in side_t:
        nb = w.shape[0] // LANES
        in_side.append(pl.BlockSpec(
            (LANES, w.shape[1]), lambda i, nb=nb: (jnp.minimum(i, nb - 1), 0)))
        out_side.append(pl.BlockSpec(
            (w.shape[1], LANES), lambda i, nb=nb: (0, jnp.minimum(i, nb - 1))))
        out_shapes.append(jax.ShapeDtypeStruct(w.shape[::-1], BF16))
    tile = pl.BlockSpec((TM_FFN, D_MODEL), lambda i: (i, 0))
    outs = pl.pallas_call(
        functools.partial(_ffn_kernel, final_norm=final_norm, n_side=len(in_side),
                          n_side_t=len(side_t)),
        grid=(steps,),
        in_specs=[tile, _resident((1, D_MODEL)), _resident((D_MODEL, D_FF)),
                  _resident((D_MODEL, D_FF)), _resident((D_FF, D_MODEL))]
                 + [_resident((1, D_MODEL))] * final_norm + in_side,
        out_specs=[tile] + out_side,
        out_shape=[jax.ShapeDtypeStruct((n, D_MODEL), F32)] + out_shapes,
        compiler_params=pltpu.CompilerParams(
            dimension_semantics=("arbitrary",), vmem_limit_bytes=VMEM_LIMIT_BYTES),
        name="ffn_final" if final_norm else "ffn",
    )(x2d, norm_g, wg, wu, wd, *([final_g] if final_norm else []), *side, *side_t)
    return outs[0], outs[1:]


def _head_rms(t, gain, bd):
    sq = (t * t).astype(BF16)
    ms = jnp.concatenate([_dot(sq[:, s * MXU_DIM:(s + 1) * MXU_DIM], bd)
                          for s in range(D_ATTN // MXU_DIM)], axis=1)
    return t * lax.rsqrt(ms + EPS) * gain


def _mixer_kernel(x_ref, g_ref, wk_ref, wu_ref, wqt_ref, wvt_ref, qg_ref, kg_ref, bd_ref, tblt_ref,
                  pw_ref, ps_ref, wouta_ref, woutp_ref, o_ref,
                  qt_scr, k_scr, vt_scr, halo_scr, d_scr, catt_scr):
    step = pl.program_id(0)
    j = step % TILES_PER_SEQ

    @pl.when(step == 0)
    def _():
        k_scr[...] = jnp.zeros(k_scr.shape, BF16)
        vt_scr[...] = jnp.zeros(vt_scr.shape, BF16)
        halo_scr[...] = jnp.zeros(halo_scr.shape, F32)

    parity = step % 2

    x = x_ref[...]
    hf = _rms(x, g_ref[...])
    h = hf.astype(BF16)
    ht = h.T
    k_scr[parity] = _head_rms(_dot(h, wk_ref[...]), kg_ref[...], bd_ref[...]).astype(BF16)
    u = _dot(h, wu_ref[...])
    qt = _dot(wqt_ref[...], ht)
    for head in range(N_HEADS):
        rows = slice(head * HEAD_DIM, (head + 1) * HEAD_DIM)
        qh = qt[rows, :]
        ms = jnp.mean(qh * qh, axis=0, keepdims=True)
        qt_scr[rows, :] = (qh * lax.rsqrt(ms + EPS) * qg_ref[rows, :]).astype(BF16)
    vt_scr[parity] = _dot(wvt_ref[...], ht).astype(BF16)
    halo = jnp.where(jnp.full((POOL_HALO, D_POOL), j, jnp.int32) == 0, 0.0, halo_scr[1 - parity])
    halo_scr[parity] = u[TM_MIX - POOL_HALO:, :]

    feat = lax.broadcasted_iota(jnp.int32, (2 * HEAD_DIM, Q_BLOCK), 0)
    pen = jnp.where(jnp.full((1, LANES), j, jnp.int32) == 0, NEG_INF, 0.0)

    def pair_scores(blk, pair):
        r0 = blk * Q_BLOCK
        n_prev = TM_MIX - r0
        ls = slice(pair * LANES, (pair + 1) * LANES)
        qt2 = qt_scr[ls, r0:r0 + Q_BLOCK]
        zero = jnp.zeros_like(qt2)
        even = jnp.where(feat < HEAD_DIM, qt2, zero)
        odd = jnp.where(feat >= HEAD_DIM, qt2, zero)
        out = []
        for half in range(Q_BLOCK // LANES):
            cl = slice(half * LANES, (half + 1) * LANES)
            qmt = jnp.concatenate([even[:, cl], odd[:, cl]], axis=1)
            lo = half * LANES
            hi = lo + BAND_GROUPS * KEY_GROUP
            out.append((_dot(k_scr[1 - parity, r0 + lo:, ls], qmt),
                        _dot(k_scr[parity, 0:hi - n_prev, ls], qmt)))
        return out

    units = [(blk, pair) for blk in range(TM_MIX // Q_BLOCK) for pair in range(N_HEADS // 2)]
    ahead = pair_scores(*units[0])
    for n, (blk, pair) in enumerate(units):
        st = ahead
        if n + 1 < len(units):
            ahead = pair_scores(*units[n + 1])
        r0 = blk * Q_BLOCK
        n_prev = TM_MIX - r0
        qcols = slice(r0, r0 + Q_BLOCK)
        for hh in range(2):
            head = 2 * pair + hh
            sl = slice(hh * LANES, (hh + 1) * LANES)
            halves = range(Q_BLOCK // LANES)

            def scores(half, grp):
                st_prev, st_cur = st[half]
                w0 = grp * KEY_GROUP
                if w0 < n_prev:
                    lo = half * LANES
                    blk_s = st_prev[w0 - lo:w0 - lo + KEY_GROUP, sl]
                else:
                    blk_s = st_cur[w0 - n_prev:w0 - n_prev + KEY_GROUP, sl]
                if _TABLE_NEEDED[grp][half]:
                    cl = slice(half * LANES, (half + 1) * LANES)
                    blk_s = blk_s + tblt_ref[head, w0:w0 + KEY_GROUP, cl]
                return blk_s

            def band(half):
                first_grp = half * LANES // KEY_GROUP
                return range(first_grp, first_grp + BAND_GROUPS)

            def group_pen(grp):
                return pen if grp * KEY_GROUP < n_prev else 0.0

            tops = []
            for half in halves:
                m = None
                for in_prev in (True, False):
                    part = [scores(half, g) for g in band(half)
                            if (g * KEY_GROUP < n_prev) == in_prev]
                    if part:
                        top = jnp.max(functools.reduce(jnp.maximum, part), axis=0, keepdims=True)
                        top = top + pen if in_prev else top
                        m = top if m is None else jnp.maximum(m, top)
                tops.append(m)
            hrows = slice(head * HEAD_DIM, (head + 1) * HEAD_DIM)
            acc = [None] * len(halves)
            ot = None
            for w_lo in range(0, WINDOW, MXU_DIM):
                cols = []
                for half in halves:
                    blocks = []
                    for grp in range(w_lo // KEY_GROUP, (w_lo + MXU_DIM) // KEY_GROUP):
                        if grp in band(half):
                            e = jnp.exp2(scores(half, grp) - (tops[half] - group_pen(grp)))
                            acc[half] = e if acc[half] is None else acc[half] + e
                            blocks.append(e.astype(BF16))
                        else:
                            blocks.append(jnp.zeros((KEY_GROUP, LANES), BF16))
                    cols.append(jnp.concatenate(blocks, axis=0))
                p_tile = jnp.concatenate(cols, axis=1)
                if w_lo < n_prev:
                    vals = vt_scr[1 - parity, hrows, r0 + w_lo:r0 + w_lo + MXU_DIM]
                else:
                    vals = vt_scr[parity, hrows, w_lo - n_prev:w_lo - n_prev + MXU_DIM]
                part = _dot(vals, p_tile)
                ot = part if ot is None else ot + part
            denom = jnp.concatenate([jnp.sum(a, axis=0, keepdims=True) for a in acc], axis=1)
            catt_scr[hrows, qcols] = (ot / denom).astype(BF16)

    pos = lax.broadcasted_iota(jnp.int32, (TM_MIX, 1), 0) + j * TM_MIX
    for g, w in enumerate(POOL_WINDOWS):
        ls = slice(g * POOL_GROUP_DIM, (g + 1) * POOL_GROUP_DIM)
        e = jnp.concatenate([halo[:, ls], u[:, ls]], axis=0)
        acc = e
        sh = 1
        while sh < w:
            acc = acc + pltpu.roll(acc, sh, axis=0)
            sh *= 2
        cnt = jnp.minimum(pos + 1, w).astype(F32)
        d = acc[POOL_HALO:, :] / cnt - u[:, ls]
        d_scr[:, ls] = d.astype(BF16)
    pooled = []
    for pr in range(D_POOL // MXU_DIM):
        sl = slice(pr * MXU_DIM, (pr + 1) * MXU_DIM)
        pooled.append((_dot(d_scr[:, sl], pw_ref[pr]) * ps_ref[:, sl]).astype(BF16))

    mixed = jnp.concatenate([catt_scr[...].T] + pooled, axis=1)
    o_ref[...] = x + _dot(mixed, jnp.concatenate([wouta_ref[...], woutp_ref[...]], axis=0))


def _mixer(x2d, norm_g, w_in_b, w_in_t, qg, kg, bd, tblt, pw_bd, ps, w_out_b):
    n = x2d.shape[0]

    def part(shape, *index):
        return pl.BlockSpec(shape, lambda i: index, pipeline_mode=pl.Buffered(1))

    return pl.pallas_call(
        _mixer_kernel,
        grid=(n // TM_MIX,),
        in_specs=[
            pl.BlockSpec((TM_MIX, D_MODEL), lambda i: (i, 0)),
            _resident((1, D_MODEL)),
            part((D_MODEL, D_ATTN), 0, 1),
            part((D_MODEL, D_POOL), 0, 3),
            part((D_ATTN, D_MODEL), 0, 0),
            part((D_ATTN, D_MODEL), 2, 0),
            _resident((D_ATTN, 1)),
            _resident((1, D_ATTN)),
            _resident((MXU_DIM, MXU_DIM)),
            _resident((N_HEADS, WINDOW, Q_BLOCK)),
            _resident((D_POOL // MXU_DIM, MXU_DIM, MXU_DIM)),
            _resident((1, D_POOL)),
            part((D_ATTN, D_MODEL), 0, 0),
            part((D_POOL, D_MODEL), 1, 0),
        ],
        out_specs=pl.BlockSpec((TM_MIX, D_MODEL), lambda i: (i, 0)),
        out_shape=jax.ShapeDtypeStruct((n, D_MODEL), F32),
        scratch_shapes=[
            pltpu.VMEM((D_ATTN, TM_MIX), BF16),
            pltpu.VMEM((2, TM_MIX, D_ATTN), BF16),
            pltpu.VMEM((2, D_ATTN, TM_MIX), BF16),
            pltpu.VMEM((2, POOL_HALO, D_POOL), F32),
            pltpu.VMEM((TM_MIX, D_POOL), BF16),
            pltpu.VMEM((D_ATTN, TM_MIX), BF16),
        ],
        compiler_params=pltpu.CompilerParams(
            dimension_semantics=("arbitrary",), vmem_limit_bytes=VMEM_LIMIT_BYTES),
        name="mixer",
    )(x2d, norm_g, w_in_b, w_in_b, w_in_t, w_in_t, qg, kg, bd, tblt, pw_bd, ps, w_out_b, w_out_b)


def _band_masks():
    w = np.arange(WINDOW)[:, None]
    q = np.arange(Q_BLOCK)[None, :]
    band = w - (q // CHUNK) * CHUNK
    in_band = (band >= 0) & (band < SPAN + CHUNK)
    far = in_band & (w - q <= SPAN - REL_CLIP)
    return in_band, far


def _table_needed():
    in_band, far = _band_masks()
    needed = []
    for grp in range(WINDOW // KEY_GROUP):
        rows = slice(grp * KEY_GROUP, (grp + 1) * KEY_GROUP)
        row = []
        for half in range(Q_BLOCK // LANES):
            cols = slice(half * LANES, (half + 1) * LANES)
            inside = half * LANES <= grp * KEY_GROUP < half * LANES + BAND_GROUPS * KEY_GROUP
            assert inside or not in_band[rows, cols].any()
            row.append(bool(inside and not far[rows, cols].all()))
        needed.append(row)
    return needed


_TABLE_NEEDED = _table_needed()


def _bias_table_t(rel_bias):
    period = 1024
    band_len = SPAN + CHUNK
    n_far = SPAN - REL_CLIP + 1
    n_near = REL_CLIP + CHUNK - 1
    rel = (rel_bias - rel_bias[:, 2 * REL_CLIP:]).astype(F32) * LOG2E
    base = jnp.concatenate([
        jnp.zeros((N_HEADS, n_far), F32),
        rel[:, 2 * REL_CLIP - 1:2 * REL_CLIP - 1 - n_near:-1],
        jnp.zeros((N_HEADS, period - n_far - n_near), F32),
    ], axis=1)
    flat = jnp.tile(base, (1, CHUNK))[:, :CHUNK * (period - 1)]
    chunk_t = flat.reshape(N_HEADS, CHUNK, period - 1)[:, :, :band_len].transpose(0, 2, 1)
    cols = [jnp.pad(chunk_t, ((0, 0), (cq * CHUNK, WINDOW - band_len - cq * CHUNK), (0, 0)),
                    constant_values=NEG_INF) for cq in range(Q_BLOCK // CHUNK)]
    return jnp.concatenate(cols, axis=2)


def _block_diag2(a, b):
    z = jnp.zeros_like(a)
    return jnp.concatenate([jnp.concatenate([a, z], axis=1),
                            jnp.concatenate([z, b], axis=1)], axis=0)


def kernel(x, ffn1_norm, ffn1_w_gate, ffn1_w_up, ffn1_w_down, mix_norm, w_in, q_norm, k_norm,
           rel_bias, pool_w, pool_scale, w_out, ffn2_norm, ffn2_w_gate, ffn2_w_up, ffn2_w_down,
           final_norm):
    b, s, d = x.shape
    assert (s, d) == (SEQ, D_MODEL) and ffn1_norm.shape[0] == 1
    x2d = x.reshape(b * s, d)
    l = 0
    head_id = jnp.arange(MXU_DIM) // HEAD_DIM
    bd = jnp.where(head_id[:, None] == head_id[None, :], 1.0 / HEAD_DIM, 0.0).astype(BF16)
    qg = (jnp.tile(q_norm[l], N_HEADS) * (HEAD_DIM ** -0.5 * LOG2E))[:, None]
    kg = jnp.tile(k_norm[l], N_HEADS)[None, :]
    pw = pool_w[l].astype(BF16)
    pw_bd = jnp.stack([_block_diag2(pw[0], pw[1]), _block_diag2(pw[2], pw[3])])
    x2d, (w_in_b, w_out_b, wg2, wu2, wd2, w_in_t) = _ffn(
        x2d, ffn1_norm[l][None, :], ffn1_w_gate[l].astype(BF16), ffn1_w_up[l].astype(BF16),
        ffn1_w_down[l].astype(BF16),
        side=(w_in[l], w_out[l], ffn2_w_gate[l], ffn2_w_up[l], ffn2_w_down[l]), side_t=(w_in[l],))
    x2d = _mixer(x2d, mix_norm[l][None, :], w_in_b, w_in_t, qg, kg, bd, _bias_table_t(rel_bias[l]),
                 pw_bd, pool_scale[l][None, :], w_out_b)
    x2d, _ = _ffn(x2d, ffn2_norm[l][None, :], wg2, wu2, wd2, final_g=final_norm[l][None, :])
    return x2d.reshape(b, s, d)
```

```python
import functools
import math

import numpy as np
import jax
import jax.numpy as jnp
from jax import lax
from jax.experimental import pallas as pl
from jax.experimental.pallas import tpu as pltpu

D_MODEL = 1024
SEQ = 4096
CHUNK = 64
N_LEFT_CHUNKS = 8
D_ATTN = 512
HEAD_DIM = 64
N_HEADS = 8
D_POOL = 512
POOL_WINDOWS = (2, 4, 8, 16)
POOL_GROUP_DIM = 128
REL_CLIP = 128
D_FF = 2816
D_IN = 3 * D_ATTN + D_POOL
EPS = 1e-6
NEG_INF = -1e30

LANES = 128
BF16_SUBLANES = 16
MXU_DIM = 256
VMEM_LIMIT_BYTES = 56 * 1024 * 1024

TM_FFN = 1024
TM_MIX = 512
TILES_PER_SEQ = SEQ // TM_MIX
Q_BLOCK = 256
SPAN = N_LEFT_CHUNKS * CHUNK
WINDOW = Q_BLOCK + SPAN
KEY_GROUP = 64
BAND_GROUPS = (SPAN + 2 * CHUNK) // KEY_GROUP
POOL_HALO = 16
LOG2E = math.log2(math.e)

BF16 = jnp.bfloat16
F32 = jnp.float32


def _rms(x, gain):
    return x * lax.rsqrt(jnp.mean(x * x, axis=-1, keepdims=True) + EPS) * gain


def _dot(a, b):
    return jnp.dot(a, b, preferred_element_type=F32)


def _ffn_kernel(*refs, final_norm, n_side, n_side_t):
    n_in = 5 + int(final_norm)
    x_ref, g_ref, wg_ref, wu_ref, wd_ref = refs[:5]
    side_in = refs[n_in:n_in + n_side]
    o_ref = refs[n_in + n_side]
    side_out = refs[n_in + n_side + 1:]
    x = x_ref[...]
    h = _rms(x, g_ref[...]).astype(BF16)
    gate = _dot(h, wg_ref[...])
    up = _dot(h, wu_ref[...])
    act = (gate * jax.nn.sigmoid(gate) * up).astype(BF16)
    y = x + 0.5 * _dot(act, wd_ref[...])
    if final_norm:
        y = _rms(y, refs[5][...])
    o_ref[...] = y
    for k, (src, dst) in enumerate(zip(side_in, side_out)):
        w = src[...]
        dst[...] = (w.T if k >= n_side - n_side_t else w).astype(BF16)


def _resident(shape):
    return pl.BlockSpec(shape, lambda i: (0,) * len(shape), pipeline_mode=pl.Buffered(1))


def _row_blocks(rows, steps):
    return max(nb for nb in range(1, steps + 1)
               if rows % nb == 0 and (rows // nb) % BF16_SUBLANES == 0)


def _ffn(x2d, norm_g, wg, wu, wd, final_g=None, side=(), side_t=()):
    n = x2d.shape[0]
    steps = n // TM_FFN
    final_norm = final_g is not None
    in_side, out_side, out_shapes = [], [], []
    for w in side:
        nb = _row_blocks(w.shape[0], steps)
        spec = pl.BlockSpec(
            (w.shape[0] // nb, w.shape[1]), lambda i, nb=nb: (jnp.minimum(i, nb - 1), 0))
        in_side.append(spec)
        out_side.append(spec)
        out_shapes.append(jax.ShapeDtypeStruct(w.shape, BF16))
    for w in side_t:
        nb = w.shape[0] // LANES
        in_side.append(pl.BlockSpec(
            (LANES, w.shape[1]), lambda i, nb=nb: (jnp.minimum(i, nb - 1), 0)))
        out_side.append(pl.BlockSpec(
            (w.shape[1], LANES), lambda i, nb=nb: (0, jnp.minimum(i, nb - 1))))
        out_shapes.append(jax.ShapeDtypeStruct(w.shape[::-1], BF16))
    tile = pl.BlockSpec((TM_FFN, D_MODEL), lambda i: (i, 0))
    outs = pl.pallas_call(
        functools.partial(_ffn_kernel, final_norm=final_norm, n_side=len(in_side),
                          n_side_t=len(side_t)),
        grid=(steps,),
        in_specs=[tile, _resident((1, D_MODEL)), _resident((D_MODEL, D_FF)),
                  _resident((D_MODEL, D_FF)), _resident((D_FF, D_MODEL))]
                 + [_resident((1, D_MODEL))] * final_norm + in_side,
        out_specs=[tile] + out_side,
        out_shape=[jax.ShapeDtypeStruct((n, D_MODEL), F32)] + out_shapes,
        compiler_params=pltpu.CompilerParams(
            dimension_semantics=("arbitrary",), vmem_limit_bytes=VMEM_LIMIT_BYTES),
        name="ffn_final" if final_norm else "ffn",
    )(x2d, norm_g, wg, wu, wd, *([final_g] if final_norm else []), *side, *side_t)
    return outs[0], outs[1:]


def _head_rms(t, gain, bd):
    sq = (t * t).astype(BF16)
    ms = jnp.concatenate([_dot(sq[:, s * MXU_DIM:(s + 1) * MXU_DIM], bd)
                          for s in range(D_ATTN // MXU_DIM)], axis=1)
    return t * lax.rsqrt(ms + EPS) * gain


def _mixer_kernel(x_ref, g_ref, wk_ref, wu_ref, wqt_ref, wvt_ref, qg_ref, kg_ref, bd_ref, tblt_ref,
                  pw_ref, ps_ref, wouta_ref, woutp_ref, o_ref,
                  qt_scr, k_scr, vt_scr, halo_scr):
    step = pl.program_id(0)
    j = step % TILES_PER_SEQ

    @pl.when(step == 0)
    def _():
        k_scr[...] = jnp.zeros(k_scr.shape, BF16)
        vt_scr[...] = jnp.zeros(vt_scr.shape, BF16)
        halo_scr[...] = jnp.zeros(halo_scr.shape, F32)

    parity = step % 2

    x = x_ref[...]
    hf = _rms(x, g_ref[...])
    h = hf.astype(BF16)
    ht = h.T
    k_scr[parity] = _head_rms(_dot(h, wk_ref[...]), kg_ref[...], bd_ref[...]).astype(BF16)
    u = _dot(h, wu_ref[...])
    qt = _dot(wqt_ref[...], ht)
    for head in range(N_HEADS):
        rows = slice(head * HEAD_DIM, (head + 1) * HEAD_DIM)
        qh = qt[rows, :]
        ms = jnp.mean(qh * qh, axis=0, keepdims=True)
        qt_scr[rows, :] = (qh * lax.rsqrt(ms + EPS) * qg_ref[rows, :]).astype(BF16)
    vt_scr[parity] = _dot(wvt_ref[...], ht).astype(BF16)
    halo = jnp.where(jnp.full((POOL_HALO, D_POOL), j, jnp.int32) == 0, 0.0, halo_scr[1 - parity])
    halo_scr[parity] = u[TM_MIX - POOL_HALO:, :]

    feat = lax.broadcasted_iota(jnp.int32, (2 * HEAD_DIM, Q_BLOCK), 0)
    pen = jnp.where(jnp.full((1, LANES), j, jnp.int32) == 0, NEG_INF, 0.0)

    def pair_scores(blk, pair):
        r0 = blk * Q_BLOCK
        n_prev = TM_MIX - r0
        ls = slice(pair * LANES, (pair + 1) * LANES)
        qt2 = qt_scr[ls, r0:r0 + Q_BLOCK]
        zero = jnp.zeros_like(qt2)
        even = jnp.where(feat < HEAD_DIM, qt2, zero)
        odd = jnp.where(feat >= HEAD_DIM, qt2, zero)
        out = []
        for half in range(Q_BLOCK // LANES):
            cl = slice(half * LANES, (half + 1) * LANES)
            qmt = jnp.concatenate([even[:, cl], odd[:, cl]], axis=1)
            lo = half * LANES
            hi = lo + BAND_GROUPS * KEY_GROUP
            out.append((_dot(k_scr[1 - parity, r0 + lo:, ls], qmt),
                        _dot(k_scr[parity, 0:hi - n_prev, ls], qmt)))
        return out

    units = [(blk, pair) for blk in range(TM_MIX // Q_BLOCK) for pair in range(N_HEADS // 2)]
    attn_t = {}
    ahead = pair_scores(*units[0])
    for n, (blk, pair) in enumerate(units):
        st = ahead
        if n + 1 < len(units):
            ahead = pair_scores(*units[n + 1])
        r0 = blk * Q_BLOCK
        n_prev = TM_MIX - r0
        for hh in range(2):
            head = 2 * pair + hh
            sl = slice(hh * LANES, (hh + 1) * LANES)
            halves = range(Q_BLOCK // LANES)

            def scores(half, grp):
                st_prev, st_cur = st[half]
                w0 = grp * KEY_GROUP
                if w0 < n_prev:
                    lo = half * LANES
                    blk_s = st_prev[w0 - lo:w0 - lo + KEY_GROUP, sl]
                else:
                    blk_s = st_cur[w0 - n_prev:w0 - n_prev + KEY_GROUP, sl]
                if _TABLE_NEEDED[grp][half]:
                    cl = slice(half * LANES, (half + 1) * LANES)
                    blk_s = blk_s + tblt_ref[head, w0:w0 + KEY_GROUP, cl]
                return blk_s

            def band(half):
                first_grp = half * LANES // KEY_GROUP
                return range(first_grp, first_grp + BAND_GROUPS)

            def group_pen(grp):
                return pen if grp * KEY_GROUP < n_prev else 0.0

            tops = []
            for half in halves:
                m = None
                for in_prev in (True, False):
                    part = [scores(half, g) for g in band(half)
                            if (g * KEY_GROUP < n_prev) == in_prev]
                    if part:
                        top = jnp.max(functools.reduce(jnp.maximum, part), axis=0, keepdims=True)
                        top = top + pen if in_prev else top
                        m = top if m is None else jnp.maximum(m, top)
                tops.append(m)
            hrows = slice(head * HEAD_DIM, (head + 1) * HEAD_DIM)
            acc = [None] * len(halves)
            ot = None
            for w_lo in range(0, WINDOW, MXU_DIM):
                cols = []
                for half in halves:
                    blocks = []
                    for grp in range(w_lo // KEY_GROUP, (w_lo + MXU_DIM) // KEY_GROUP):
                        if grp in band(half):
                            e = jnp.exp2(scores(half, grp) - (tops[half] - group_pen(grp)))
                            acc[half] = e if acc[half] is None else acc[half] + e
                            blocks.append(e.astype(BF16))
                        else:
                            blocks.append(jnp.zeros((KEY_GROUP, LANES), BF16))
                    cols.append(jnp.concatenate(blocks, axis=0))
                p_tile = jnp.concatenate(cols, axis=1)
                if w_lo < n_prev:
                    vals = vt_scr[1 - parity, hrows, r0 + w_lo:r0 + w_lo + MXU_DIM]
                else:
                    vals = vt_scr[parity, hrows, w_lo - n_prev:w_lo - n_prev + MXU_DIM]
                part = _dot(vals, p_tile)
                ot = part if ot is None else ot + part
            denom = jnp.concatenate([jnp.sum(a, axis=0, keepdims=True) for a in acc], axis=1)
            attn_t[head, blk] = (ot / denom).astype(BF16)

    pos = lax.broadcasted_iota(jnp.int32, (TM_MIX, 1), 0) + j * TM_MIX
    diffs = []
    for g, w in enumerate(POOL_WINDOWS):
        ls = slice(g * POOL_GROUP_DIM, (g + 1) * POOL_GROUP_DIM)
        e = jnp.concatenate([halo[:, ls], u[:, ls]], axis=0)
        acc = e
        sh = 1
        while sh < w:
            acc = acc + pltpu.roll(acc, sh, axis=0)
            sh *= 2
        cnt = jnp.minimum(pos + 1, w).astype(F32)
        d = acc[POOL_HALO:, :] / cnt - u[:, ls]
        diffs.append(d.astype(BF16))
    pooled = []
    for pr in range(D_POOL // MXU_DIM):
        sl = slice(pr * MXU_DIM, (pr + 1) * MXU_DIM)
        pair_d = jnp.concatenate(diffs[2 * pr:2 * pr + 2], axis=1)
        pooled.append((_dot(pair_d, pw_ref[pr]) * ps_ref[:, sl]).astype(BF16))

    attn = jnp.concatenate(
        [jnp.concatenate([attn_t[head, blk] for blk in range(TM_MIX // Q_BLOCK)], axis=1)
         for head in range(N_HEADS)], axis=0)
    mixed = jnp.concatenate([attn.T] + pooled, axis=1)
    o_ref[...] = x + _dot(mixed, jnp.concatenate([wouta_ref[...], woutp_ref[...]], axis=0))


def _mixer(x2d, norm_g, w_in_b, w_in_t, qg, kg, bd, tblt, pw_bd, ps, w_out_b):
    n = x2d.shape[0]

    def part(shape, *index):
        return pl.BlockSpec(shape, lambda i: index, pipeline_mode=pl.Buffered(1))

    return pl.pallas_call(
        _mixer_kernel,
        grid=(n // TM_MIX,),
        in_specs=[
            pl.BlockSpec((TM_MIX, D_MODEL), lambda i: (i, 0)),
            _resident((1, D_MODEL)),
            part((D_MODEL, D_ATTN), 0, 1),
            part((D_MODEL, D_POOL), 0, 3),
            part((D_ATTN, D_MODEL), 0, 0),
            part((D_ATTN, D_MODEL), 2, 0),
            _resident((D_ATTN, 1)),
            _resident((1, D_ATTN)),
            _resident((MXU_DIM, MXU_DIM)),
            _resident((N_HEADS, WINDOW, Q_BLOCK)),
            _resident((D_POOL // MXU_DIM, MXU_DIM, MXU_DIM)),
            _resident((1, D_POOL)),
            part((D_ATTN, D_MODEL), 0, 0),
            part((D_POOL, D_MODEL), 1, 0),
        ],
        out_specs=pl.BlockSpec((TM_MIX, D_MODEL), lambda i: (i, 0)),
        out_shape=jax.ShapeDtypeStruct((n, D_MODEL), F32),
        scratch_shapes=[
            pltpu.VMEM((D_ATTN, TM_MIX), BF16),
            pltpu.VMEM((2, TM_MIX, D_ATTN), BF16),
            pltpu.VMEM((2, D_ATTN, TM_MIX), BF16),
            pltpu.VMEM((2, POOL_HALO, D_POOL), F32),
        ],
        compiler_params=pltpu.CompilerParams(
            dimension_semantics=("arbitrary",), vmem_limit_bytes=VMEM_LIMIT_BYTES),
        name="mixer",
    )(x2d, norm_g, w_in_b, w_in_b, w_in_t, w_in_t, qg, kg, bd, tblt, pw_bd, ps, w_out_b, w_out_b)


def _band_masks():
    w = np.arange(WINDOW)[:, None]
    q = np.arange(Q_BLOCK)[None, :]
    band = w - (q // CHUNK) * CHUNK
    in_band = (band >= 0) & (band < SPAN + CHUNK)
    far = in_band & (w - q <= SPAN - REL_CLIP)
    return in_band, far


def _table_needed():
    in_band, far = _band_masks()
    needed = []
    for grp in range(WINDOW // KEY_GROUP):
        rows = slice(grp * KEY_GROUP, (grp + 1) * KEY_GROUP)
        row = []
        for half in range(Q_BLOCK // LANES):
            cols = slice(half * LANES, (half + 1) * LANES)
            inside = half * LANES <= grp * KEY_GROUP < half * LANES + BAND_GROUPS * KEY_GROUP
            assert inside or not in_band[rows, cols].any()
            row.append(bool(inside and not far[rows, cols].all()))
        needed.append(row)
    return needed


_TABLE_NEEDED = _table_needed()


def _bias_table_t(rel_bias):
    period = 1024
    band_len = SPAN + CHUNK
    n_far = SPAN - REL_CLIP + 1
    n_near = REL_CLIP + CHUNK - 1
    rel = (rel_bias - rel_bias[:, 2 * REL_CLIP:]).astype(F32) * LOG2E
    base = jnp.concatenate([
        jnp.zeros((N_HEADS, n_far), F32),
        rel[:, 2 * REL_CLIP - 1:2 * REL_CLIP - 1 - n_near:-1],
        jnp.zeros((N_HEADS, period - n_far - n_near), F32),
    ], axis=1)
    flat = jnp.tile(base, (1, CHUNK))[:, :CHUNK * (period - 1)]
    chunk_t = flat.reshape(N_HEADS, CHUNK, period - 1)[:, :, :band_len].transpose(0, 2, 1)
    cols = [jnp.pad(chunk_t, ((0, 0), (cq * CHUNK, WINDOW - band_len - cq * CHUNK), (0, 0)),
                    constant_values=NEG_INF) for cq in range(Q_BLOCK // CHUNK)]
    return jnp.concatenate(cols, axis=2)


def _block_diag2(a, b):
    z = jnp.zeros_like(a)
    return jnp.concatenate([jnp.concatenate([a, z], axis=1),
                            jnp.concatenate([z, b], axis=1)], axis=0)


def kernel(x, ffn1_norm, ffn1_w_gate, ffn1_w_up, ffn1_w_down, mix_norm, w_in, q_norm, k_norm,
           rel_bias, pool_w, pool_scale, w_out, ffn2_norm, ffn2_w_gate, ffn2_w_up, ffn2_w_down,
           final_norm):
    b, s, d = x.shape
    assert (s, d) == (SEQ, D_MODEL) and ffn1_norm.shape[0] == 1
    x2d = x.reshape(b * s, d)
    l = 0
    head_id = jnp.arange(MXU_DIM) // HEAD_DIM
    bd = jnp.where(head_id[:, None] == head_id[None, :], 1.0 / HEAD_DIM, 0.0).astype(BF16)
    qg = (jnp.tile(q_norm[l], N_HEADS) * (HEAD_DIM ** -0.5 * LOG2E))[:, None]
    kg = jnp.tile(k_norm[l], N_HEADS)[None, :]
    pw = pool_w[l].astype(BF16)
    pw_bd = jnp.stack([_block_diag2(pw[0], pw[1]), _block_diag2(pw[2], pw[3])])
    x2d, (w_in_b, w_out_b, wg2, wu2, wd2, w_in_t) = _ffn(
        x2d, ffn1_norm[l][None, :], ffn1_w_gate[l].astype(BF16), ffn1_w_up[l].astype(BF16),
        ffn1_w_down[l].astype(BF16),
        side=(w_in[l], w_out[l], ffn2_w_gate[l], ffn2_w_up[l], ffn2_w_down[l]), side_t=(w_in[l],))
    x2d = _mixer(x2d, mix_norm[l][None, :], w_in_b, w_in_t, qg, kg, bd, _bias_table_t(rel_bias[l]),
                 pw_bd, pool_scale[l][None, :], w_out_b)
    x2d, _ = _ffn(x2d, ffn2_norm[l][None, :], wg2, wu2, wd2, final_g=final_norm[l][None, :])
    return x2d.reshape(b, s, d)
```

```python
import functools
import math

import numpy as np
import jax
import jax.numpy as jnp
from jax import lax
from jax.experimental import pallas as pl
from jax.experimental.pallas import tpu as pltpu

D_MODEL = 1024
SEQ = 4096
CHUNK = 64
N_LEFT_CHUNKS = 8
D_ATTN = 512
HEAD_DIM = 64
N_HEADS = 8
D_POOL = 512
POOL_WINDOWS = (2, 4, 8, 16)
POOL_GROUP_DIM = 128
REL_CLIP = 128
D_FF = 2816
D_IN = 3 * D_ATTN + D_POOL
EPS = 1e-6
NEG_INF = -1e30

LANES = 128
BF16_SUBLANES = 16
MXU_DIM = 256
VMEM_LIMIT_BYTES = 56 * 1024 * 1024

TM_FFN = 1024
TM_MIX = 1024
TILES_PER_SEQ = SEQ // TM_MIX
Q_BLOCK = 256
SPAN = N_LEFT_CHUNKS * CHUNK
WINDOW = Q_BLOCK + SPAN
KEY_GROUP = 64
BAND_GROUPS = (SPAN + 2 * CHUNK) // KEY_GROUP
POOL_HALO = 16
LOG2E = math.log2(math.e)

BF16 = jnp.bfloat16
F32 = jnp.float32


def _rms(x, gain):
    return x * lax.rsqrt(jnp.mean(x * x, axis=-1, keepdims=True) + EPS) * gain


def _dot(a, b):
    return jnp.dot(a, b, preferred_element_type=F32)


def _ffn_kernel(*refs, final_norm, n_side, n_side_t):
    n_in = 5 + int(final_norm)
    x_ref, g_ref, wg_ref, wu_ref, wd_ref = refs[:5]
    side_in = refs[n_in:n_in + n_side]
    o_ref = refs[n_in + n_side]
    side_out = refs[n_in + n_side + 1:]
    x = x_ref[...]
    h = _rms(x, g_ref[...]).astype(BF16)
    gate = _dot(h, wg_ref[...])
    up = _dot(h, wu_ref[...])
    act = (gate * jax.nn.sigmoid(gate) * up).astype(BF16)
    y = x + 0.5 * _dot(act, wd_ref[...])
    if final_norm:
        y = _rms(y, refs[5][...])
    o_ref[...] = y
    for k, (src, dst) in enumerate(zip(side_in, side_out)):
        w = src[...]
        dst[...] = (w.T if k >= n_side - n_side_t else w).astype(BF16)


def _resident(shape):
    return pl.BlockSpec(shape, lambda i: (0,) * len(shape), pipeline_mode=pl.Buffered(1))


def _row_blocks(rows, steps):
    return max(nb for nb in range(1, steps + 1)
               if rows % nb == 0 and (rows // nb) % BF16_SUBLANES == 0)


def _ffn(x2d, norm_g, wg, wu, wd, final_g=None, side=(), side_t=()):
    n = x2d.shape[0]
    steps = n // TM_FFN
    final_norm = final_g is not None
    in_side, out_side, out_shapes = [], [], []
    for w in side:
        nb = _row_blocks(w.shape[0], steps)
        spec = pl.BlockSpec(
            (w.shape[0] // nb, w.shape[1]), lambda i, nb=nb: (jnp.minimum(i, nb - 1), 0))
        in_side.append(spec)
        out_side.append(spec)
        out_shapes.append(jax.ShapeDtypeStruct(w.shape, BF16))
    for w in side_t:
        nb = w.shape[0] // LANES
        in_side.append(pl.BlockSpec(
            (LANES, w.shape[1]), lambda i, nb=nb: (jnp.minimum(i, nb - 1), 0)))
        out_side.append(pl.BlockSpec(
            (w.shape[1], LANES), lambda i, nb=nb: (0, jnp.minimum(i, nb - 1))))
        out_shapes.append(jax.ShapeDtypeStruct(w.shape[::-1], BF16))
    tile = pl.BlockSpec((TM_FFN, D_MODEL), lambda i: (i, 0))
    outs = pl.pallas_call(
        functools.partial(_ffn_kernel, final_norm=final_norm, n_side=len(in_side),
                          n_side_t=len(side_t)),
        grid=(steps,),
        in_specs=[tile, _resident((1, D_MODEL)), _resident((D_MODEL, D_FF)),
                  _resident((D_MODEL, D_FF)), _resident((D_FF, D_MODEL))]
                 + [_resident((1, D_MODEL))] * final_norm + in_side,
        out_specs=[tile] + out_side,
        out_shape=[jax.ShapeDtypeStruct((n, D_MODEL), F32)] + out_shapes,
        compiler_params=pltpu.CompilerParams(
            dimension_semantics=("arbitrary",), vmem_limit_bytes=VMEM_LIMIT_BYTES),
        name="ffn_final" if final_norm else "ffn",
    )(x2d, norm_g, wg, wu, wd, *([final_g] if final_norm else []), *side, *side_t)
    return outs[0], outs[1:]


def _head_rms(t, gain, bd):
    sq = (t * t).astype(BF16)
    ms = jnp.concatenate([_dot(sq[:, s * MXU_DIM:(s + 1) * MXU_DIM], bd)
                          for s in range(D_ATTN // MXU_DIM)], axis=1)
    return t * lax.rsqrt(ms + EPS) * gain


def _mixer_kernel(x_ref, g_ref, wk_ref, wu_ref, wqt_ref, wvt_ref, qg_ref, kg_ref, bd_ref, tblt_ref,
                  pw_ref, ps_ref, wouta_ref, woutp_ref, o_ref,
                  qt_scr, k_scr, vt_scr, halo_scr):
    step = pl.program_id(0)
    j = step % TILES_PER_SEQ

    @pl.when(step == 0)
    def _():
        k_scr[...] = jnp.zeros(k_scr.shape, BF16)
        vt_scr[...] = jnp.zeros(vt_scr.shape, BF16)
        halo_scr[...] = jnp.zeros(halo_scr.shape, F32)

    parity = step % 2

    x = x_ref[...]
    hf = _rms(x, g_ref[...])
    h = hf.astype(BF16)
    ht = h.T
    k_scr[parity] = _head_rms(_dot(h, wk_ref[...]), kg_ref[...], bd_ref[...]).astype(BF16)
    u = _dot(h, wu_ref[...])
    qt = _dot(wqt_ref[...], ht)
    for head in range(N_HEADS):
        rows = slice(head * HEAD_DIM, (head + 1) * HEAD_DIM)
        qh = qt[rows, :]
        ms = jnp.mean(qh * qh, axis=0, keepdims=True)
        qt_scr[rows, :] = (qh * lax.rsqrt(ms + EPS) * qg_ref[rows, :]).astype(BF16)
    vt_scr[parity] = _dot(wvt_ref[...], ht).astype(BF16)
    halo = jnp.where(jnp.full((POOL_HALO, D_POOL), j, jnp.int32) == 0, 0.0, halo_scr[1 - parity])
    halo_scr[parity] = u[TM_MIX - POOL_HALO:, :]

    feat = lax.broadcasted_iota(jnp.int32, (2 * HEAD_DIM, Q_BLOCK), 0)
    pen = jnp.where(jnp.full((1, LANES), j, jnp.int32) == 0, NEG_INF, 0.0)

    def pair_scores(blk, pair):
        r0 = blk * Q_BLOCK
        n_prev = max(SPAN - r0, 0)
        c0 = max(r0 - SPAN, 0)
        ls = slice(pair * LANES, (pair + 1) * LANES)
        qt2 = qt_scr[ls, r0:r0 + Q_BLOCK]
        zero = jnp.zeros_like(qt2)
        even = jnp.where(feat < HEAD_DIM, qt2, zero)
        odd = jnp.where(feat >= HEAD_DIM, qt2, zero)
        out = []
        for half in range(Q_BLOCK // LANES):
            cl = slice(half * LANES, (half + 1) * LANES)
            qmt = jnp.concatenate([even[:, cl], odd[:, cl]], axis=1)
            lo = half * LANES
            hi = lo + BAND_GROUPS * KEY_GROUP
            prev = (_dot(k_scr[1 - parity, TM_MIX - n_prev + lo:, ls], qmt)
                    if lo < n_prev else None)
            cur_lo = max(lo, n_prev)
            out.append((prev, _dot(k_scr[parity, c0 + cur_lo - n_prev:c0 + hi - n_prev, ls], qmt)))
        return out

    units = [(blk, pair) for blk in range(TM_MIX // Q_BLOCK) for pair in range(N_HEADS // 2)]
    attn_t = {}
    ahead = pair_scores(*units[0])
    for n, (blk, pair) in enumerate(units):
        st = ahead
        if n + 1 < len(units):
            ahead = pair_scores(*units[n + 1])
        r0 = blk * Q_BLOCK
        n_prev = max(SPAN - r0, 0)
        c0 = max(r0 - SPAN, 0)
        for hh in range(2):
            head = 2 * pair + hh
            sl = slice(hh * LANES, (hh + 1) * LANES)
            halves = range(Q_BLOCK // LANES)

            def scores(half, grp):
                st_prev, st_cur = st[half]
                w0 = grp * KEY_GROUP
                if w0 < n_prev:
                    lo = half * LANES
                    blk_s = st_prev[w0 - lo:w0 - lo + KEY_GROUP, sl]
                else:
                    cur_lo = max(half * LANES, n_prev)
                    blk_s = st_cur[w0 - cur_lo:w0 - cur_lo + KEY_GROUP, sl]
                if _TABLE_NEEDED[grp][half]:
                    cl = slice(half * LANES, (half + 1) * LANES)
                    blk_s = blk_s + tblt_ref[head, w0:w0 + KEY_GROUP, cl]
                return blk_s

            def band(half):
                first_grp = half * LANES // KEY_GROUP
                return range(first_grp, first_grp + BAND_GROUPS)

            def group_pen(grp):
                return pen if grp * KEY_GROUP < n_prev else 0.0

            tops = []
            for half in halves:
                m = None
                for in_prev in (True, False):
                    part = [scores(half, g) for g in band(half)
                            if (g * KEY_GROUP < n_prev) == in_prev]
                    if part:
                        top = jnp.max(functools.reduce(jnp.maximum, part), axis=0, keepdims=True)
                        top = top + pen if in_prev else top
                        m = top if m is None else jnp.maximum(m, top)
                tops.append(m)
            hrows = slice(head * HEAD_DIM, (head + 1) * HEAD_DIM)
            acc = [None] * len(halves)
            ot = None
            for w_lo in range(0, WINDOW, MXU_DIM):
                cols = []
                for half in halves:
                    blocks = []
                    for grp in range(w_lo // KEY_GROUP, (w_lo + MXU_DIM) // KEY_GROUP):
                        if grp in band(half):
                            e = jnp.exp2(scores(half, grp) - (tops[half] - group_pen(grp)))
                            acc[half] = e if acc[half] is None else acc[half] + e
                            blocks.append(e.astype(BF16))
                        else:
                            blocks.append(jnp.zeros((KEY_GROUP, LANES), BF16))
                    cols.append(jnp.concatenate(blocks, axis=0))
                p_tile = jnp.concatenate(cols, axis=1)
                if w_lo < n_prev:
                    v0 = TM_MIX - n_prev + w_lo
                    vals = vt_scr[1 - parity, hrows, v0:v0 + MXU_DIM]
                else:
                    v0 = c0 + w_lo - n_prev
                    vals = vt_scr[parity, hrows, v0:v0 + MXU_DIM]
                part = _dot(vals, p_tile)
                ot = part if ot is None else ot + part
            denom = jnp.concatenate([jnp.sum(a, axis=0, keepdims=True) for a in acc], axis=1)
            attn_t[head, blk] = (ot / denom).astype(BF16)

    pos = lax.broadcasted_iota(jnp.int32, (TM_MIX, 1), 0) + j * TM_MIX
    diffs = []
    for g, w in enumerate(POOL_WINDOWS):
        ls = slice(g * POOL_GROUP_DIM, (g + 1) * POOL_GROUP_DIM)
        e = jnp.concatenate([halo[:, ls], u[:, ls]], axis=0)
        acc = e
        sh = 1
        while sh < w:
            acc = acc + pltpu.roll(acc, sh, axis=0)
            sh *= 2
        cnt = jnp.minimum(pos + 1, w).astype(F32)
        d = acc[POOL_HALO:, :] / cnt - u[:, ls]
        diffs.append(d.astype(BF16))
    pooled = []
    for pr in range(D_POOL // MXU_DIM):
        sl = slice(pr * MXU_DIM, (pr + 1) * MXU_DIM)
        pair_d = jnp.concatenate(diffs[2 * pr:2 * pr + 2], axis=1)
        pooled.append((_dot(pair_d, pw_ref[pr]) * ps_ref[:, sl]).astype(BF16))

    attn = jnp.concatenate(
        [jnp.concatenate([attn_t[head, blk] for blk in range(TM_MIX // Q_BLOCK)], axis=1)
         for head in range(N_HEADS)], axis=0)
    mixed = jnp.concatenate([attn.T] + pooled, axis=1)
    o_ref[...] = x + _dot(mixed, jnp.concatenate([wouta_ref[...], woutp_ref[...]], axis=0))


def _mixer(x2d, norm_g, w_in_b, w_in_t, qg, kg, bd, tblt, pw_bd, ps, w_out_b):
    n = x2d.shape[0]

    def part(shape, *index):
        return pl.BlockSpec(shape, lambda i: index, pipeline_mode=pl.Buffered(1))

    return pl.pallas_call(
        _mixer_kernel,
        grid=(n // TM_MIX,),
        in_specs=[
            pl.BlockSpec((TM_MIX, D_MODEL), lambda i: (i, 0)),
            _resident((1, D_MODEL)),
            part((D_MODEL, D_ATTN), 0, 1),
            part((D_MODEL, D_POOL), 0, 3),
            part((D_ATTN, D_MODEL), 0, 0),
            part((D_ATTN, D_MODEL), 2, 0),
            _resident((D_ATTN, 1)),
            _resident((1, D_ATTN)),
            _resident((MXU_DIM, MXU_DIM)),
            _resident((N_HEADS, WINDOW, Q_BLOCK)),
            _resident((D_POOL // MXU_DIM, MXU_DIM, MXU_DIM)),
            _resident((1, D_POOL)),
            part((D_ATTN, D_MODEL), 0, 0),
            part((D_POOL, D_MODEL), 1, 0),
        ],
        out_specs=pl.BlockSpec((TM_MIX, D_MODEL), lambda i: (i, 0)),
        out_shape=jax.ShapeDtypeStruct((n, D_MODEL), F32),
        scratch_shapes=[
            pltpu.VMEM((D_ATTN, TM_MIX), BF16),
            pltpu.VMEM((2, TM_MIX, D_ATTN), BF16),
            pltpu.VMEM((2, D_ATTN, TM_MIX), BF16),
            pltpu.VMEM((2, POOL_HALO, D_POOL), F32),
        ],
        compiler_params=pltpu.CompilerParams(
            dimension_semantics=("arbitrary",), vmem_limit_bytes=VMEM_LIMIT_BYTES),
        name="mixer",
    )(x2d, norm_g, w_in_b, w_in_b, w_in_t, w_in_t, qg, kg, bd, tblt, pw_bd, ps, w_out_b, w_out_b)


def _band_masks():
    w = np.arange(WINDOW)[:, None]
    q = np.arange(Q_BLOCK)[None, :]
    band = w - (q // CHUNK) * CHUNK
    in_band = (band >= 0) & (band < SPAN + CHUNK)
    far = in_band & (w - q <= SPAN - REL_CLIP)
    return in_band, far


def _table_needed():
    in_band, far = _band_masks()
    needed = []
    for grp in range(WINDOW // KEY_GROUP):
        rows = slice(grp * KEY_GROUP, (grp + 1) * KEY_GROUP)
        row = []
        for half in range(Q_BLOCK // LANES):
            cols = slice(half * LANES, (half + 1) * LANES)
            inside = half * LANES <= grp * KEY_GROUP < half * LANES + BAND_GROUPS * KEY_GROUP
            assert inside or not in_band[rows, cols].any()
            row.append(bool(inside and not far[rows, cols].all()))
        needed.append(row)
    return needed


_TABLE_NEEDED = _table_needed()


def _bias_table_t(rel_bias):
    period = 1024
    band_len = SPAN + CHUNK
    n_far = SPAN - REL_CLIP + 1
    n_near = REL_CLIP + CHUNK - 1
    rel = (rel_bias - rel_bias[:, 2 * REL_CLIP:]).astype(F32) * LOG2E
    base = jnp.concatenate([
        jnp.zeros((N_HEADS, n_far), F32),
        rel[:, 2 * REL_CLIP - 1:2 * REL_CLIP - 1 - n_near:-1],
        jnp.zeros((N_HEADS, period - n_far - n_near), F32),
    ], axis=1)
    flat = jnp.tile(base, (1, CHUNK))[:, :CHUNK * (period - 1)]
    chunk_t = flat.reshape(N_HEADS, CHUNK, period - 1)[:, :, :band_len].transpose(0, 2, 1)
    cols = [jnp.pad(chunk_t, ((0, 0), (cq * CHUNK, WINDOW - band_len - cq * CHUNK), (0, 0)),
                    constant_values=NEG_INF) for cq in range(Q_BLOCK // CHUNK)]
    return jnp.concatenate(cols, axis=2)


def _block_diag2(a, b):
    z = jnp.zeros_like(a)
    return jnp.concatenate([jnp.concatenate([a, z], axis=1),
                            jnp.concatenate([z, b], axis=1)], axis=0)


def kernel(x, ffn1_norm, ffn1_w_gate, ffn1_w_up, ffn1_w_down, mix_norm, w_in, q_norm, k_norm,
           rel_bias, pool_w, pool_scale, w_out, ffn2_norm, ffn2_w_gate, ffn2_w_up, ffn2_w_down,
           final_norm):
    b, s, d = x.shape
    assert (s, d) == (SEQ, D_MODEL) and ffn1_norm.shape[0] == 1
    x2d = x.reshape(b * s, d)
    l = 0
    head_id = jnp.arange(MXU_DIM) // HEAD_DIM
    bd = jnp.where(head_id[:, None] == head_id[None, :], 1.0 / HEAD_DIM, 0.0).astype(BF16)
    qg = (jnp.tile(q_norm[l], N_HEADS) * (HEAD_DIM ** -0.5 * LOG2E))[:, None]
    kg = jnp.tile(k_norm[l], N_HEADS)[None, :]
    pw = pool_w[l].astype(BF16)
    pw_bd = jnp.stack([_block_diag2(pw[0], pw[1]), _block_diag2(pw[2], pw[3])])
    x2d, (w_in_b, w_out_b, wg2, wu2, wd2, w_in_t) = _ffn(
        x2d, ffn1_norm[l][None, :], ffn1_w_gate[l].astype(BF16), ffn1_w_up[l].astype(BF16),
        ffn1_w_down[l].astype(BF16),
        side=(w_in[l], w_out[l], ffn2_w_gate[l], ffn2_w_up[l], ffn2_w_down[l]), side_t=(w_in[l],))
    x2d = _mixer(x2d, mix_norm[l][None, :], w_in_b, w_in_t, qg, kg, bd, _bias_table_t(rel_bias[l]),
                 pw_bd, pool_scale[l][None, :], w_out_b)
    x2d, _ = _ffn(x2d, ffn2_norm[l][None, :], wg2, wu2, wd2, final_g=final_norm[l][None, :])
    return x2d.reshape(b, s, d)
```

```python
import functools
import math

import numpy as np
import jax
import jax.numpy as jnp
from jax import lax
from jax.experimental import pallas as pl
from jax.experimental.pallas import tpu as pltpu

D_MODEL = 1024
SEQ = 4096
CHUNK = 64
N_LEFT_CHUNKS = 8
D_ATTN = 512
HEAD_DIM = 64
N_HEADS = 8
D_POOL = 512
POOL_WINDOWS = (2, 4, 8, 16)
POOL_GROUP_DIM = 128
REL_CLIP = 128
D_FF = 2816
D_IN = 3 * D_ATTN + D_POOL
EPS = 1e-6
NEG_INF = -1e30

LANES = 128
BF16_SUBLANES = 16
MXU_DIM = 256
VMEM_LIMIT_BYTES = 56 * 1024 * 1024

TM_FFN = 1024
X_SLOTS = 3
TM_MIX = 1024
TILES_PER_SEQ = SEQ // TM_MIX
Q_BLOCK = 256
SPAN = N_LEFT_CHUNKS * CHUNK
WINDOW = Q_BLOCK + SPAN
KEY_GROUP = 64
BAND_GROUPS = (SPAN + 2 * CHUNK) // KEY_GROUP
POOL_HALO = 16
LOG2E = math.log2(math.e)

BF16 = jnp.bfloat16
F32 = jnp.float32


def _rms(x, gain):
    return x * lax.rsqrt(jnp.mean(x * x, axis=-1, keepdims=True) + EPS) * gain


def _dot(a, b):
    return jnp.dot(a, b, preferred_element_type=F32)


def _x_copy(x_hbm, xbuf, sem, t):
    return pltpu.make_async_copy(x_hbm.at[pl.ds(t * TM_FFN, TM_FFN), :], xbuf.at[t % X_SLOTS],
                                 sem.at[t % X_SLOTS])


def _ffn_kernel(*refs, final_norm, n_side, n_side_t, stream_x):
    n_in = 5 + int(final_norm)
    x_ref, g_ref, wg_ref, wu_ref, wd_ref = refs[:5]
    side_in = refs[n_in:n_in + n_side]
    o_ref = refs[n_in + n_side]
    side_out = refs[n_in + n_side + 1:n_in + 2 * n_side + 1]
    if stream_x:
        xbuf, sem = refs[n_in + 2 * n_side + 1:]
        step = pl.program_id(0)

        @pl.when(step == 0)
        def _():
            for t in range(X_SLOTS - 1):
                _x_copy(x_ref, xbuf, sem, t).start()

        @pl.when(step + X_SLOTS - 1 < pl.num_programs(0))
        def _():
            _x_copy(x_ref, xbuf, sem, step + X_SLOTS - 1).start()

        _x_copy(x_ref, xbuf, sem, step).wait()
        x = xbuf[step % X_SLOTS]
    else:
        x = x_ref[...]
    h = _rms(x, g_ref[...]).astype(BF16)
    gate = _dot(h, wg_ref[...])
    up = _dot(h, wu_ref[...])
    act = (gate * jax.nn.sigmoid(gate) * up).astype(BF16)
    y = x + 0.5 * _dot(act, wd_ref[...])
    if final_norm:
        y = _rms(y, refs[5][...])
    o_ref[...] = y
    for k, (src, dst) in enumerate(zip(side_in, side_out)):
        w = src[...]
        dst[...] = (w.T if k >= n_side - n_side_t else w).astype(BF16)


def _resident(shape):
    return pl.BlockSpec(shape, lambda i: (0,) * len(shape), pipeline_mode=pl.Buffered(1))


def _row_blocks(rows, steps):
    return max(nb for nb in range(1, steps + 1)
               if rows % nb == 0 and (rows // nb) % BF16_SUBLANES == 0)


def _ffn(x2d, norm_g, wg, wu, wd, final_g=None, side=(), side_t=()):
    n = x2d.shape[0]
    steps = n // TM_FFN
    final_norm = final_g is not None
    in_side, out_side, out_shapes = [], [], []
    for w in side:
        nb = _row_blocks(w.shape[0], steps)
        spec = pl.BlockSpec(
            (w.shape[0] // nb, w.shape[1]), lambda i, nb=nb: (jnp.minimum(i, nb - 1), 0))
        in_side.append(spec)
        out_side.append(spec)
        out_shapes.append(jax.ShapeDtypeStruct(w.shape, BF16))
    for w in side_t:
        nb = w.shape[0] // LANES
        in_side.append(pl.BlockSpec(
            (LANES, w.shape[1]), lambda i, nb=nb: (jnp.minimum(i, nb - 1), 0)))
        out_side.append(pl.BlockSpec(
            (w.shape[1], LANES), lambda i, nb=nb: (0, jnp.minimum(i, nb - 1))))
        out_shapes.append(jax.ShapeDtypeStruct(w.shape[::-1], BF16))
    tile = pl.BlockSpec((TM_FFN, D_MODEL), lambda i: (i, 0))
    stream_x = not (side or side_t)
    tile_in = pl.BlockSpec(memory_space=pl.ANY) if stream_x else tile
    scratch = [pltpu.VMEM((X_SLOTS, TM_FFN, D_MODEL), F32),
               pltpu.SemaphoreType.DMA((X_SLOTS,))] if stream_x else []
    outs = pl.pallas_call(
        functools.partial(_ffn_kernel, final_norm=final_norm, n_side=len(in_side),
                          n_side_t=len(side_t), stream_x=stream_x),
        grid=(steps,),
        scratch_shapes=scratch,
        in_specs=[tile_in, _resident((1, D_MODEL)), _resident((D_MODEL, D_FF)),
                  _resident((D_MODEL, D_FF)), _resident((D_FF, D_MODEL))]
                 + [_resident((1, D_MODEL))] * final_norm + in_side,
        out_specs=[tile] + out_side,
        out_shape=[jax.ShapeDtypeStruct((n, D_MODEL), F32)] + out_shapes,
        compiler_params=pltpu.CompilerParams(
            dimension_semantics=("arbitrary",), vmem_limit_bytes=VMEM_LIMIT_BYTES),
        name="ffn_final" if final_norm else "ffn",
    )(x2d, norm_g, wg, wu, wd, *([final_g] if final_norm else []), *side, *side_t)
    return outs[0], outs[1:]


def _head_rms(t, gain, bd):
    sq = (t * t).astype(BF16)
    ms = jnp.concatenate([_dot(sq[:, s * MXU_DIM:(s + 1) * MXU_DIM], bd)
                          for s in range(D_ATTN // MXU_DIM)], axis=1)
    return t * lax.rsqrt(ms + EPS) * gain


def _mixer_kernel(x_ref, g_ref, wk_ref, wu_ref, wqt_ref, wvt_ref, qg_ref, kg_ref, bd_ref, tblt_ref,
                  pw_ref, ps_ref, wouta_ref, woutp_ref, o_ref,
                  qt_scr, k_scr, vt_scr, halo_scr):
    step = pl.program_id(0)
    j = step % TILES_PER_SEQ

    @pl.when(step == 0)
    def _():
        k_scr[...] = jnp.zeros(k_scr.shape, BF16)
        vt_scr[...] = jnp.zeros(vt_scr.shape, BF16)
        halo_scr[...] = jnp.zeros(halo_scr.shape, F32)

    parity = step % 2

    x = x_ref[...]
    hf = _rms(x, g_ref[...])
    h = hf.astype(BF16)
    ht = h.T
    k_scr[parity] = _head_rms(_dot(h, wk_ref[...]), kg_ref[...], bd_ref[...]).astype(BF16)
    u = _dot(h, wu_ref[...])
    qt = _dot(wqt_ref[...], ht)
    for head in range(N_HEADS):
        rows = slice(head * HEAD_DIM, (head + 1) * HEAD_DIM)
        qh = qt[rows, :]
        ms = jnp.mean(qh * qh, axis=0, keepdims=True)
        qt_scr[rows, :] = (qh * lax.rsqrt(ms + EPS) * qg_ref[rows, :]).astype(BF16)
    vt_scr[parity] = _dot(wvt_ref[...], ht).astype(BF16)
    halo = jnp.where(jnp.full((POOL_HALO, D_POOL), j, jnp.int32) == 0, 0.0, halo_scr[1 - parity])
    halo_scr[parity] = u[TM_MIX - POOL_HALO:, :]

    feat = lax.broadcasted_iota(jnp.int32, (2 * HEAD_DIM, Q_BLOCK), 0)
    pen = jnp.where(jnp.full((1, LANES), j, jnp.int32) == 0, NEG_INF, 0.0)

    def pair_scores(blk, pair):
        r0 = blk * Q_BLOCK
        n_prev = max(SPAN - r0, 0)
        c0 = max(r0 - SPAN, 0)
        ls = slice(pair * LANES, (pair + 1) * LANES)
        qt2 = qt_scr[ls, r0:r0 + Q_BLOCK]
        zero = jnp.zeros_like(qt2)
        even = jnp.where(feat < HEAD_DIM, qt2, zero)
        odd = jnp.where(feat >= HEAD_DIM, qt2, zero)
        out = []
        for half in range(Q_BLOCK // LANES):
            cl = slice(half * LANES, (half + 1) * LANES)
            qmt = jnp.concatenate([even[:, cl], odd[:, cl]], axis=1)
            lo = half * LANES
            hi = lo + BAND_GROUPS * KEY_GROUP
            prev = (_dot(k_scr[1 - parity, TM_MIX - n_prev + lo:, ls], qmt)
                    if lo < n_prev else None)
            cur_lo = max(lo, n_prev)
            out.append((prev, _dot(k_scr[parity, c0 + cur_lo - n_prev:c0 + hi - n_prev, ls], qmt)))
        return out

    units = [(blk, pair) for blk in range(TM_MIX // Q_BLOCK) for pair in range(N_HEADS // 2)]
    attn_t = {}
    ahead = pair_scores(*units[0])
    for n, (blk, pair) in enumerate(units):
        st = ahead
        if n + 1 < len(units):
            ahead = pair_scores(*units[n + 1])
        r0 = blk * Q_BLOCK
        n_prev = max(SPAN - r0, 0)
        c0 = max(r0 - SPAN, 0)
        for hh in range(2):
            head = 2 * pair + hh
            sl = slice(hh * LANES, (hh + 1) * LANES)
            halves = range(Q_BLOCK // LANES)

            def scores(half, grp):
                st_prev, st_cur = st[half]
                w0 = grp * KEY_GROUP
                if w0 < n_prev:
                    lo = half * LANES
                    blk_s = st_prev[w0 - lo:w0 - lo + KEY_GROUP, sl]
                else:
                    cur_lo = max(half * LANES, n_prev)
                    blk_s = st_cur[w0 - cur_lo:w0 - cur_lo + KEY_GROUP, sl]
                if _TABLE_NEEDED[grp][half]:
                    cl = slice(half * LANES, (half + 1) * LANES)
                    blk_s = blk_s + tblt_ref[head, w0:w0 + KEY_GROUP, cl]
                return blk_s

            def band(half):
                first_grp = half * LANES // KEY_GROUP
                return range(first_grp, first_grp + BAND_GROUPS)

            def group_pen(grp):
                return pen if grp * KEY_GROUP < n_prev else 0.0

            tops = []
            for half in halves:
                m = None
                for in_prev in (True, False):
                    part = [scores(half, g) for g in band(half)
                            if (g * KEY_GROUP < n_prev) == in_prev]
                    if part:
                        top = jnp.max(functools.reduce(jnp.maximum, part), axis=0, keepdims=True)
                        top = top + pen if in_prev else top
                        m = top if m is None else jnp.maximum(m, top)
                tops.append(m)
            hrows = slice(head * HEAD_DIM, (head + 1) * HEAD_DIM)
            acc = [None] * len(halves)
            ot = None
            for w_lo in range(0, WINDOW, MXU_DIM):
                cols = []
                for half in halves:
                    blocks = []
                    for grp in range(w_lo // KEY_GROUP, (w_lo + MXU_DIM) // KEY_GROUP):
                        if grp in band(half):
                            e = jnp.exp2(scores(half, grp) - (tops[half] - group_pen(grp)))
                            acc[half] = e if acc[half] is None else acc[half] + e
                            blocks.append(e.astype(BF16))
                        else:
                            blocks.append(jnp.zeros((KEY_GROUP, LANES), BF16))
                    cols.append(jnp.concatenate(blocks, axis=0))
                p_tile = jnp.concatenate(cols, axis=1)
                if w_lo < n_prev:
                    v0 = TM_MIX - n_prev + w_lo
                    vals = vt_scr[1 - parity, hrows, v0:v0 + MXU_DIM]
                else:
                    v0 = c0 + w_lo - n_prev
                    vals = vt_scr[parity, hrows, v0:v0 + MXU_DIM]
                part = _dot(vals, p_tile)
                ot = part if ot is None else ot + part
            denom = jnp.concatenate([jnp.sum(a, axis=0, keepdims=True) for a in acc], axis=1)
            attn_t[head, blk] = (ot / denom).astype(BF16)

    pos = lax.broadcasted_iota(jnp.int32, (TM_MIX, 1), 0) + j * TM_MIX
    diffs = []
    for g, w in enumerate(POOL_WINDOWS):
        ls = slice(g * POOL_GROUP_DIM, (g + 1) * POOL_GROUP_DIM)
        e = jnp.concatenate([halo[:, ls], u[:, ls]], axis=0)
        acc = e
        sh = 1
        while sh < w:
            acc = acc + pltpu.roll(acc, sh, axis=0)
            sh *= 2
        cnt = jnp.minimum(pos + 1, w).astype(F32)
        d = acc[POOL_HALO:, :] / cnt - u[:, ls]
        diffs.append(d.astype(BF16))
    pooled = []
    for pr in range(D_POOL // MXU_DIM):
        sl = slice(pr * MXU_DIM, (pr + 1) * MXU_DIM)
        pair_d = jnp.concatenate(diffs[2 * pr:2 * pr + 2], axis=1)
        pooled.append((_dot(pair_d, pw_ref[pr]) * ps_ref[:, sl]).astype(BF16))

    attn = jnp.concatenate(
        [jnp.concatenate([attn_t[head, blk] for blk in range(TM_MIX // Q_BLOCK)], axis=1)
         for head in range(N_HEADS)], axis=0)
    mixed = jnp.concatenate([attn.T] + pooled, axis=1)
    o_ref[...] = x + _dot(mixed, jnp.concatenate([wouta_ref[...], woutp_ref[...]], axis=0))


def _mixer(x2d, norm_g, w_in_b, w_in_t, qg, kg, bd, tblt, pw_bd, ps, w_out_b):
    n = x2d.shape[0]

    def part(shape, *index):
        return pl.BlockSpec(shape, lambda i: index, pipeline_mode=pl.Buffered(1))

    return pl.pallas_call(
        _mixer_kernel,
        grid=(n // TM_MIX,),
        in_specs=[
            pl.BlockSpec((TM_MIX, D_MODEL), lambda i: (i, 0)),
            _resident((1, D_MODEL)),
            part((D_MODEL, D_ATTN), 0, 1),
            part((D_MODEL, D_POOL), 0, 3),
            part((D_ATTN, D_MODEL), 0, 0),
            part((D_ATTN, D_MODEL), 2, 0),
            _resident((D_ATTN, 1)),
            _resident((1, D_ATTN)),
            _resident((MXU_DIM, MXU_DIM)),
            _resident((N_HEADS, WINDOW, Q_BLOCK)),
            _resident((D_POOL // MXU_DIM, MXU_DIM, MXU_DIM)),
            _resident((1, D_POOL)),
            part((D_ATTN, D_MODEL), 0, 0),
            part((D_POOL, D_MODEL), 1, 0),
        ],
        out_specs=pl.BlockSpec((TM_MIX, D_MODEL), lambda i: (i, 0)),
        out_shape=jax.ShapeDtypeStruct((n, D_MODEL), F32),
        scratch_shapes=[
            pltpu.VMEM((D_ATTN, TM_MIX), BF16),
            pltpu.VMEM((2, TM_MIX, D_ATTN), BF16),
            pltpu.VMEM((2, D_ATTN, TM_MIX), BF16),
            pltpu.VMEM((2, POOL_HALO, D_POOL), F32),
        ],
        compiler_params=pltpu.CompilerParams(
            dimension_semantics=("arbitrary",), vmem_limit_bytes=VMEM_LIMIT_BYTES),
        name="mixer",
    )(x2d, norm_g, w_in_b, w_in_b, w_in_t, w_in_t, qg, kg, bd, tblt, pw_bd, ps, w_out_b, w_out_b)


def _band_masks():
    w = np.arange(WINDOW)[:, None]
    q = np.arange(Q_BLOCK)[None, :]
    band = w - (q // CHUNK) * CHUNK
    in_band = (band >= 0) & (band < SPAN + CHUNK)
    far = in_band & (w - q <= SPAN - REL_CLIP)
    return in_band, far


def _table_needed():
    in_band, far = _band_masks()
    needed = []
    for grp in range(WINDOW // KEY_GROUP):
        rows = slice(grp * KEY_GROUP, (grp + 1) * KEY_GROUP)
        row = []
        for half in range(Q_BLOCK // LANES):
            cols = slice(half * LANES, (half + 1) * LANES)
            inside = half * LANES <= grp * KEY_GROUP < half * LANES + BAND_GROUPS * KEY_GROUP
            assert inside or not in_band[rows, cols].any()
            row.append(bool(inside and not far[rows, cols].all()))
        needed.append(row)
    return needed


_TABLE_NEEDED = _table_needed()


def _bias_table_t(rel_bias):
    period = 1024
    band_len = SPAN + CHUNK
    n_far = SPAN - REL_CLIP + 1
    n_near = REL_CLIP + CHUNK - 1
    rel = (rel_bias - rel_bias[:, 2 * REL_CLIP:]).astype(F32) * LOG2E
    base = jnp.concatenate([
        jnp.zeros((N_HEADS, n_far), F32),
        rel[:, 2 * REL_CLIP - 1:2 * REL_CLIP - 1 - n_near:-1],
        jnp.zeros((N_HEADS, period - n_far - n_near), F32),
    ], axis=1)
    flat = jnp.tile(base, (1, CHUNK))[:, :CHUNK * (period - 1)]
    chunk_t = flat.reshape(N_HEADS, CHUNK, period - 1)[:, :, :band_len].transpose(0, 2, 1)
    cols = [jnp.pad(chunk_t, ((0, 0), (cq * CHUNK, WINDOW - band_len - cq * CHUNK), (0, 0)),
                    constant_values=NEG_INF) for cq in range(Q_BLOCK // CHUNK)]
    return jnp.concatenate(cols, axis=2)


def _block_diag2(a, b):
    z = jnp.zeros_like(a)
    return jnp.concatenate([jnp.concatenate([a, z], axis=1),
                            jnp.concatenate([z, b], axis=1)], axis=0)


def kernel(x, ffn1_norm, ffn1_w_gate, ffn1_w_up, ffn1_w_down, mix_norm, w_in, q_norm, k_norm,
           rel_bias, pool_w, pool_scale, w_out, ffn2_norm, ffn2_w_gate, ffn2_w_up, ffn2_w_down,
           final_norm):
    b, s, d = x.shape
    assert (s, d) == (SEQ, D_MODEL) and ffn1_norm.shape[0] == 1
    x2d = x.reshape(b * s, d)
    l = 0
    head_id = jnp.arange(MXU_DIM) // HEAD_DIM
    bd = jnp.where(head_id[:, None] == head_id[None, :], 1.0 / HEAD_DIM, 0.0).astype(BF16)
    qg = (jnp.tile(q_norm[l], N_HEADS) * (HEAD_DIM ** -0.5 * LOG2E))[:, None]
    kg = jnp.tile(k_norm[l], N_HEADS)[None, :]
    pw = pool_w[l].astype(BF16)
    pw_bd = jnp.stack([_block_diag2(pw[0], pw[1]), _block_diag2(pw[2], pw[3])])
    x2d, (w_in_b, w_out_b, wg2, wu2, wd2, w_in_t) = _ffn(
        x2d, ffn1_norm[l][None, :], ffn1_w_gate[l].astype(BF16), ffn1_w_up[l].astype(BF16),
        ffn1_w_down[l].astype(BF16),
        side=(w_in[l], w_out[l], ffn2_w_gate[l], ffn2_w_up[l], ffn2_w_down[l]), side_t=(w_in[l],))
    x2d = _mixer(x2d, mix_norm[l][None, :], w_in_b, w_in_t, qg, kg, bd, _bias_table_t(rel_bias[l]),
                 pw_bd, pool_scale[l][None, :], w_out_b)
    x2d, _ = _ffn(x2d, ffn2_norm[l][None, :], wg2, wu2, wd2, final_g=final_norm[l][None, :])
    return x2d.reshape(b, s, d)
```
